```python
import jax, jax.numpy as jnp
from jax import lax
import numpy as np

D_MODEL = 2048
BATCH = 1
SEQ = 8192
DEPTH = 1

N_MEM = 256
WIDTH_A = D_MODEL
WIDTH_B = D_MODEL
MIX_WIDTH = WIDTH_A + WIDTH_B
HEAD_DIM = 128
N_HEADS_B = WIDTH_B // HEAD_DIM
CONV_WIDTH_A = 3
CONV_WIDTH_B = 4
LRU_C = 8.0
XATTN_HEADS = 4
XATTN_HEAD_DIM = D_MODEL // XATTN_HEADS
IN_COLS = 4 * WIDTH_A + 2 * WIDTH_B
RMS_EPS = 1e-6

kernel_name = "hymba_shortconv_rglru_memxattn"


def rms_norm(x, g):
    xf = x.astype(jnp.float32)
    y = xf * lax.rsqrt(jnp.mean(xf * xf, axis=-1, keepdims=True) + RMS_EPS)
    return (y * g.astype(jnp.float32)).astype(x.dtype)


def causal_depthwise_conv(u, w, b):
    k_width = w.shape[0]
    s = u.shape[1]
    up = jnp.pad(u, ((0, 0), (k_width - 1, 0), (0, 0)))
    y = b
    for k in range(k_width):
        y = y + up[:, k:k + s] * w[k]
    return y


def _lin_rec_combine(c1, c2):
    a1, b1 = c1
    a2, b2 = c2
    return a1 * a2, a2 * b1 + b2


def rg_lru(xc, w_r, b_r, w_i, b_i, lam):
    bsz, s, _ = xc.shape
    xh = xc.reshape(bsz, s, N_HEADS_B, HEAD_DIM)
    r_pre = jnp.einsum('bshi,hij->bshj', xh, w_r).reshape(bsz, s, WIDTH_B) + b_r
    i_pre = jnp.einsum('bshi,hij->bshj', xh, w_i).reshape(bsz, s, WIDTH_B) + b_i
    r = jax.nn.sigmoid(r_pre.astype(jnp.float32))
    i = jax.nn.sigmoid(i_pre.astype(jnp.float32))
    log_a = -LRU_C * r * jax.nn.softplus(-lam.astype(jnp.float32))
    a = jnp.exp(log_a)
    mult = jnp.sqrt(-jnp.expm1(2.0 * log_a))
    mult = mult.at[:, 0].set(1.0)
    u = mult * i * xc.astype(jnp.float32)
    _, h = lax.associative_scan(_lin_rec_combine, (a, u), axis=1)
    return h.astype(xc.dtype)


def memory_cross_attention(hn, mn, w_q, w_kv, w_o):
    bsz, s, _ = hn.shape
    m = mn.shape[1]
    q = (hn @ w_q).reshape(bsz, s, XATTN_HEADS, XATTN_HEAD_DIM)
    kv = mn @ w_kv
    k = kv[..., :D_MODEL].reshape(bsz, m, XATTN_HEADS, XATTN_HEAD_DIM)
    v = kv[..., D_MODEL:].reshape(bsz, m, XATTN_HEADS, XATTN_HEAD_DIM)
    scores = jnp.einsum('bshd,bmhd->bhsm', q, k).astype(jnp.float32) * (XATTN_HEAD_DIM ** -0.5)
    p = jax.nn.softmax(scores, axis=-1).astype(v.dtype)
    o = jnp.einsum('bhsm,bmhd->bshd', p, v).reshape(bsz, s, D_MODEL)
    return o @ w_o


def setup_inputs(seed: int = 0) -> dict:
    key = jax.random.key(seed)
    ks = jax.random.split(key, 24)
    f32 = jnp.float32
    nrm = lambda k, shape, scale: jax.random.normal(k, shape, f32) * scale
    a_base = jax.random.uniform(ks[10], (DEPTH, WIDTH_B), f32, 0.9, 0.999)
    return {
        "x": nrm(ks[0], (BATCH, SEQ, D_MODEL), 1.0),
        "mem": nrm(ks[1], (BATCH, N_MEM, D_MODEL), 1.0),
        "norm_mix_g": 1.0 + nrm(ks[2], (DEPTH, D_MODEL), 0.02),
        "w_in": nrm(ks[3], (DEPTH, D_MODEL, IN_COLS), D_MODEL ** -0.5),
        "conv_a_w": nrm(ks[4], (DEPTH, CONV_WIDTH_A, WIDTH_A), CONV_WIDTH_A ** -0.5),
        "conv_a_b": nrm(ks[5], (DEPTH, WIDTH_A), 0.01),
        "conv_b_w": nrm(ks[6], (DEPTH, CONV_WIDTH_B, WIDTH_B), CONV_WIDTH_B ** -0.5),
        "conv_b_b": nrm(ks[7], (DEPTH, WIDTH_B), 0.01),
        "w_rgate": nrm(ks[8], (DEPTH, N_HEADS_B, HEAD_DIM, HEAD_DIM), HEAD_DIM ** -0.5),
        "b_rgate": nrm(ks[9], (DEPTH, WIDTH_B), 0.01),
        "w_igate": nrm(ks[11], (DEPTH, N_HEADS_B, HEAD_DIM, HEAD_DIM), HEAD_DIM ** -0.5),
        "b_igate": nrm(ks[12], (DEPTH, WIDTH_B), 0.01),
        "lru_lambda": jnp.log(a_base) - jnp.log1p(-a_base),
        "w_out": nrm(ks[13], (DEPTH, MIX_WIDTH, D_MODEL), MIX_WIDTH ** -0.5),
        "norm_x_g": 1.0 + nrm(ks[14], (DEPTH, D_MODEL), 0.02),
        "norm_mem_g": 1.0 + nrm(ks[15], (DEPTH, D_MODEL), 0.02),
        "w_q": nrm(ks[16], (DEPTH, D_MODEL, D_MODEL), D_MODEL ** -0.5),
        "w_kv": nrm(ks[17], (DEPTH, D_MODEL, 2 * D_MODEL), D_MODEL ** -0.5),
        "w_o": nrm(ks[18], (DEPTH, D_MODEL, D_MODEL), D_MODEL ** -0.5),
        "norm_f_g": 1.0 + nrm(ks[19], (D_MODEL,), 0.02),
    }


def reference(x, mem, norm_mix_g, w_in, conv_a_w, conv_a_b, conv_b_w, conv_b_b,
              w_rgate, b_rgate, w_igate, b_igate, lru_lambda, w_out,
              norm_x_g, norm_mem_g, w_q, w_kv, w_o, norm_f_g):
    h = x
    for l in range(DEPTH):
        hn = rms_norm(h, norm_mix_g[l])
        proj = hn @ w_in[l]
        v_a, b_a, c_a, g_a, x_b, g_b = jnp.split(
            proj, np.cumsum([WIDTH_A] * 4 + [WIDTH_B])[:].tolist(), axis=-1)
        y_a = b_a * causal_depthwise_conv(c_a * v_a, conv_a_w[l], conv_a_b[l])
        y_a = y_a * jax.nn.silu(g_a)
        x_c = causal_depthwise_conv(x_b, conv_b_w[l], conv_b_b[l])
        y_b = rg_lru(x_c, w_rgate[l], b_rgate[l], w_igate[l], b_igate[l], lru_lambda[l])
        y_b = y_b * jax.nn.silu(g_b)
        h = h + jnp.concatenate([y_a, y_b], axis=-1) @ w_out[l]
        h = h + memory_cross_attention(rms_norm(h, norm_x_g[l]), rms_norm(mem, norm_mem_g[l]),
                                       w_q[l], w_kv[l], w_o[l])
    return rms_norm(h, norm_f_g)
```

```python
import functools

import jax
import jax.numpy as jnp
from jax import lax
from jax.experimental import pallas as pl
from jax.experimental.pallas import tpu as pltpu

D_MODEL = 2048
HEAD_DIM = 128
LRU_C = 8.0
XATTN_HEADS = 4
XATTN_HEAD_DIM = D_MODEL // XATTN_HEADS
RMS_EPS = 1e-6

SUBLANES = 8
VMEM_LIMIT_BYTES = 56 * 1024 * 1024

MIX_SEQ_TILE = 256
MIX_CH_TILE = 512
ATTN_SEQ_TILE = 256
KV_COL_TILE = 512


def _rms_norm(xf, g):
    ms = jnp.mean(xf * xf, axis=-1, keepdims=True)
    return xf * lax.rsqrt(ms + RMS_EPS) * g


def _dot(a, b):
    return jnp.dot(a, b, preferred_element_type=jnp.float32)


def _kv_kernel(mem_ref, g_ref, wkv_ref, kv_ref):
    mn = _rms_norm(mem_ref[...], g_ref[...]).astype(jnp.bfloat16)
    kv_ref[...] = _dot(mn, wkv_ref[...]).astype(kv_ref.dtype)


def _shift_rows(u, tail, s, row8):
    rolled = pltpu.roll(u, s, axis=0)
    head = jnp.where(row8 < s, pltpu.roll(tail, s, axis=0), rolled[0:SUBLANES])
    return jnp.concatenate([head, rolled[SUBLANES:]], axis=0)


def _linear_scan(a, b, h_prev):
    t_rows, ch = a.shape
    row_in_group = lax.broadcasted_iota(jnp.int32, (t_rows, ch), 0) % SUBLANES
    for s in (1, 2, 4):
        keep = row_in_group >= s
        a_sh = pltpu.roll(a, s, axis=0)
        b_sh = pltpu.roll(b, s, axis=0)
        b = jnp.where(keep, a * b_sh + b, b)
        a = jnp.where(keep, a * a_sh, a)
    outs = []
    for g in range(t_rows // SUBLANES):
        sl = slice(g * SUBLANES, (g + 1) * SUBLANES)
        h_g = a[sl] * h_prev + b[sl]
        outs.append(h_g)
        h_prev = jnp.broadcast_to(h_g[SUBLANES - 1:SUBLANES, :], (SUBLANES, ch))
    return jnp.concatenate(outs, axis=0), h_prev


def _mixer_kernel(x_ref, g_ref, wv_ref, wb_ref, wc_ref, wga_ref, wxb_ref, wgb_ref,
                  caw_ref, cab_ref, cbw_ref, cbb_ref, wri_ref, br_ref, bi_ref, lam_ref,
                  ya_ref, yb_ref, cv_tail, xb_tail, h_carry):
    i = pl.program_id(1)
    t_rows, ch = ya_ref.shape

    @pl.when(i == 0)
    def _():
        cv_tail[...] = jnp.zeros_like(cv_tail)
        xb_tail[...] = jnp.zeros_like(xb_tail)
        h_carry[...] = jnp.zeros_like(h_carry)

    hn = _rms_norm(x_ref[...], g_ref[...]).astype(jnp.bfloat16)
    row8 = lax.broadcasted_iota(jnp.int32, (SUBLANES, ch), 0)

    cv = _dot(hn, wc_ref[...]) * _dot(hn, wv_ref[...])
    tail = cv_tail[...]
    caw = caw_ref[...]
    conv = (cab_ref[...] + caw[2:3] * cv
            + caw[1:2] * _shift_rows(cv, tail, 1, row8)
            + caw[0:1] * _shift_rows(cv, tail, 2, row8))
    cv_tail[...] = cv[t_rows - SUBLANES:]
    g_a = _dot(hn, wga_ref[...])
    y_a = _dot(hn, wb_ref[...]) * conv * (g_a * jax.nn.sigmoid(g_a))
    ya_ref[...] = y_a.astype(ya_ref.dtype)

    xb = _dot(hn, wxb_ref[...])
    tail = xb_tail[...]
    cbw = cbw_ref[...]
    xc = (cbb_ref[...] + cbw[3:4] * xb
          + cbw[2:3] * _shift_rows(xb, tail, 1, row8)
          + cbw[1:2] * _shift_rows(xb, tail, 2, row8)
          + cbw[0:1] * _shift_rows(xb, tail, 3, row8))
    xb_tail[...] = xb[t_rows - SUBLANES:]

    xc_bf = xc.astype(jnp.bfloat16)
    r_parts, i_parts = [], []
    for hh in range(ch // HEAD_DIM):
        ri = _dot(xc_bf[:, hh * HEAD_DIM:(hh + 1) * HEAD_DIM], wri_ref[hh])
        r_parts.append(ri[:, :HEAD_DIM])
        i_parts.append(ri[:, HEAD_DIM:])
    r = jax.nn.sigmoid(jnp.concatenate(r_parts, axis=1) + br_ref[...])
    i_gate = jax.nn.sigmoid(jnp.concatenate(i_parts, axis=1) + bi_ref[...])

    neg_lam = -lam_ref[...]
    softplus_neg_lam = jnp.maximum(neg_lam, 0.0) + jnp.log1p(jnp.exp(-jnp.abs(neg_lam)))
    log_a = (-LRU_C * softplus_neg_lam) * r
    a = jnp.exp(log_a)
    th = jnp.tanh(log_a)
    mult = jnp.sqrt(-2.0 * th / (1.0 - th))
    first_row = (lax.broadcasted_iota(jnp.int32, (t_rows, ch), 0) == 0) & (i == 0)
    mult = jnp.where(first_row, 1.0, mult)
    u = mult * i_gate * xc

    h, h_last = _linear_scan(a, u, h_carry[...])
    h_carry[...] = h_last
    g_b = _dot(hn, wgb_ref[...])
    yb_ref[...] = (h * (g_b * jax.nn.sigmoid(g_b))).astype(yb_ref.dtype)


def _attn_kernel(x_ref, ya_ref, yb_ref, woa_ref, wob_ref, gx_ref, wq_ref, k_ref, v_ref,
                 wo_ref, gf_ref, out_ref):
    h = x_ref[...] + _dot(ya_ref[...], woa_ref[...]) + _dot(yb_ref[...], wob_ref[...])
    hn = _rms_norm(h, gx_ref[...]).astype(jnp.bfloat16)
    q = _dot(hn, wq_ref[...]).astype(jnp.bfloat16)
    scale = XATTN_HEAD_DIM ** -0.5
    o_parts = []
    for hd in range(XATTN_HEADS):
        sl = slice(hd * XATTN_HEAD_DIM, (hd + 1) * XATTN_HEAD_DIM)
        s = lax.dot_general(q[:, sl], k_ref[:, sl], (((1,), (1,)), ((), ())),
                            preferred_element_type=jnp.float32) * scale
        p = jnp.exp(s - jnp.max(s, axis=-1, keepdims=True))
        p = p / jnp.sum(p, axis=-1, keepdims=True)
        o_parts.append(_dot(p.astype(jnp.bfloat16), v_ref[:, sl]))
    o = jnp.concatenate(o_parts, axis=1).astype(jnp.bfloat16)
    h2 = h + _dot(o, wo_ref[...])
    out_ref[...] = _rms_norm(h2, gf_ref[...]).astype(out_ref.dtype)


def _resident(shape):
    return pl.BlockSpec(shape, lambda *_: (0,) * len(shape), pipeline_mode=pl.Buffered(1))


def _layer(h, mem, norm_mix_g, w_in, conv_a_w, conv_a_b, conv_b_w, conv_b_b, w_rgate, b_rgate,
           w_igate, b_igate, lru_lambda, w_out, norm_x_g, norm_mem_g, w_q, w_kv, w_o, norm_f_g):
    seq, d = h.shape
    n_mem = mem.shape[0]
    bf16 = jnp.bfloat16
    row = lambda v: v.reshape(1, -1)

    kv = pl.pallas_call(
        _kv_kernel,
        grid=(2 * d // KV_COL_TILE,),
        in_specs=[pl.BlockSpec((n_mem, d), lambda j: (0, 0)),
                  pl.BlockSpec((1, d), lambda j: (0, 0)),
                  pl.BlockSpec((d, KV_COL_TILE), lambda j: (0, j))],
        out_specs=pl.BlockSpec((n_mem, KV_COL_TILE), lambda j: (0, j)),
        out_shape=jax.ShapeDtypeStruct((n_mem, 2 * d), bf16),
        compiler_params=pltpu.CompilerParams(dimension_semantics=("arbitrary",),
                                             vmem_limit_bytes=VMEM_LIMIT_BYTES),
        name="kv_proj",
    )(mem, row(norm_mem_g), w_kv.astype(bf16))

    tm, cb = MIX_SEQ_TILE, MIX_CH_TILE
    n_cb = d // cb
    w_in_bf = w_in.astype(bf16)
    w_ri = jnp.concatenate([w_rgate, w_igate], axis=-1).astype(bf16)
    col = lambda g: pl.BlockSpec((d, cb), lambda j, i, g=g: (0, g * n_cb + j))
    chan = lambda r: pl.BlockSpec((r, cb), lambda j, i: (0, j))
    y_a, y_b = pl.pallas_call(
        _mixer_kernel,
        grid=(n_cb, seq // tm),
        in_specs=[pl.BlockSpec((tm, d), lambda j, i: (i, 0)),
                  pl.BlockSpec((1, d), lambda j, i: (0, 0)),
                  col(0), col(1), col(2), col(3), col(4), col(5),
                  chan(conv_a_w.shape[0]), chan(1), chan(conv_b_w.shape[0]), chan(1),
                  pl.BlockSpec((cb // HEAD_DIM, HEAD_DIM, 2 * HEAD_DIM), lambda j, i: (j, 0, 0)),
                  chan(1), chan(1), chan(1)],
        out_specs=[pl.BlockSpec((tm, cb), lambda j, i: (i, j)),
                   pl.BlockSpec((tm, cb), lambda j, i: (i, j))],
        out_shape=[jax.ShapeDtypeStruct((seq, d), bf16), jax.ShapeDtypeStruct((seq, d), bf16)],
        scratch_shapes=[pltpu.VMEM((SUBLANES, cb), jnp.float32),
                        pltpu.VMEM((SUBLANES, cb), jnp.float32),
                        pltpu.VMEM((SUBLANES, cb), jnp.float32)],
        compiler_params=pltpu.CompilerParams(dimension_semantics=("arbitrary", "arbitrary"),
                                             vmem_limit_bytes=VMEM_LIMIT_BYTES),
        name="mixer",
    )(h, row(norm_mix_g), w_in_bf, w_in_bf, w_in_bf, w_in_bf, w_in_bf, w_in_bf,
      conv_a_w, row(conv_a_b), conv_b_w, row(conv_b_b), w_ri,
      row(b_rgate), row(b_igate), row(lru_lambda))

    ta = ATTN_SEQ_TILE
    w_out_bf = w_out.astype(bf16)
    out = pl.pallas_call(
        _attn_kernel,
        grid=(seq // ta,),
        in_specs=[pl.BlockSpec((ta, d), lambda i: (i, 0)),
                  pl.BlockSpec((ta, d), lambda i: (i, 0)),
                  pl.BlockSpec((ta, d), lambda i: (i, 0)),
                  pl.BlockSpec((d, d), lambda i: (0, 0), pipeline_mode=pl.Buffered(1)),
                  pl.BlockSpec((d, d), lambda i: (1, 0), pipeline_mode=pl.Buffered(1)),
                  _resident((1, d)),
                  _resident((d, d)),
                  pl.BlockSpec((n_mem, d), lambda i: (0, 0), pipeline_mode=pl.Buffered(1)),
                  pl.BlockSpec((n_mem, d), lambda i: (0, 1), pipeline_mode=pl.Buffered(1)),
                  _resident((d, d)),
                  _resident((1, d))],
        out_specs=pl.BlockSpec((ta, d), lambda i: (i, 0)),
        out_shape=jax.ShapeDtypeStruct((seq, d), h.dtype),
        compiler_params=pltpu.CompilerParams(dimension_semantics=("arbitrary",),
                                             vmem_limit_bytes=VMEM_LIMIT_BYTES),
        name="outproj_xattn",
    )(h, y_a, y_b, w_out_bf, w_out_bf, row(norm_x_g), w_q.astype(bf16), kv, kv,
      w_o.astype(bf16), row(norm_f_g))
    return out


def kernel(x, mem, norm_mix_g, w_in, conv_a_w, conv_a_b, conv_b_w, conv_b_b, w_rgate, b_rgate,
           w_igate, b_igate, lru_lambda, w_out, norm_x_g, norm_mem_g, w_q, w_kv, w_o, norm_f_g):
    bsz, seq, d = x.shape
    depth = w_in.shape[0]
    assert bsz == 1 and depth == 1 and d == D_MODEL
    out = _layer(x[0], mem[0], norm_mix_g[0], w_in[0], conv_a_w[0], conv_a_b[0], conv_b_w[0],
                 conv_b_b[0], w_rgate[0], b_rgate[0], w_igate[0], b_igate[0], lru_lambda[0],
                 w_out[0], norm_x_g[0], norm_mem_g[0], w_q[0], w_kv[0], w_o[0], norm_f_g)
    return out[None]
```

```python
import jax
import jax.numpy as jnp
from jax import lax
from jax.experimental import pallas as pl
from jax.experimental.pallas import tpu as pltpu

D_MODEL = 2048
HEAD_DIM = 128
LRU_C = 8.0
XATTN_HEADS = 4
XATTN_HEAD_DIM = D_MODEL // XATTN_HEADS
RMS_EPS = 1e-6

SUBLANES = 8
VMEM_LIMIT_BYTES = 56 * 1024 * 1024

MIX_SEQ_TILE = 512
MIX_ROW_CHUNK = 256
MIX_CH_TILE = 512
ATTN_SEQ_TILE = 256
KV_COL_TILE = 512


def _rms_norm(xf, g):
    ms = jnp.mean(xf * xf, axis=-1, keepdims=True)
    return xf * lax.rsqrt(ms + RMS_EPS) * g


def _dot(a, b):
    return jnp.dot(a, b, preferred_element_type=jnp.float32)


def _kv_kernel(mem_ref, g_ref, wkv_ref, kv_ref):
    mn = _rms_norm(mem_ref[...], g_ref[...]).astype(jnp.bfloat16)
    kv_ref[...] = _dot(mn, wkv_ref[...]).astype(kv_ref.dtype)


def _group_rows(u):
    return u.reshape(u.shape[0] // SUBLANES, SUBLANES, u.shape[1])


def _shift_rows(u, tail, s):
    rows, ch = u.shape
    rolled = pltpu.roll(_group_rows(u), s, axis=1)
    before = jnp.concatenate([pltpu.roll(tail, s, axis=0)[None], rolled[:-1]], axis=0)
    row = lax.broadcasted_iota(jnp.int32, rolled.shape, 1)
    return jnp.where(row < s, before, rolled).reshape(rows, ch)


def _linear_scan(a, b, h_prev):
    t_rows, ch = a.shape
    a, b = _group_rows(a), _group_rows(b)
    row = lax.broadcasted_iota(jnp.int32, a.shape, 1)
    for s in (1, 2, 4):
        keep = row >= s
        b = a * jnp.where(keep, pltpu.roll(b, s, axis=1), 0.0) + b
        a = a * jnp.where(keep, pltpu.roll(a, s, axis=1), 1.0)
    outs = []
    for g in range(t_rows // SUBLANES):
        h_g = a[g] * h_prev + b[g]
        outs.append(h_g)
        h_prev = jnp.broadcast_to(h_g[SUBLANES - 1:SUBLANES, :], (SUBLANES, ch))
    return jnp.concatenate(outs, axis=0), h_prev


def _mixer_chunk(x, g, w_refs, caw, cab, cbw, cbb, wri_ref, b_r, b_i, lam_scale, carries,
                 is_first_tile_chunk):
    wv_ref, wb_ref, wc_ref, wga_ref, wxb_ref, wgb_ref = w_refs
    cv_tail, xb_tail, h_prev = carries
    rows, ch = x.shape[0], cab.shape[1]
    hn = _rms_norm(x, g).astype(jnp.bfloat16)

    xb = _dot(hn, wxb_ref[...])
    xc = (cbb + cbw[3:4] * xb
          + cbw[2:3] * _shift_rows(xb, xb_tail, 1)
          + cbw[1:2] * _shift_rows(xb, xb_tail, 2)
          + cbw[0:1] * _shift_rows(xb, xb_tail, 3))
    xc_bf = xc.astype(jnp.bfloat16)
    r_parts, i_parts = [], []
    for hh in range(ch // HEAD_DIM):
        ri = _dot(xc_bf[:, hh * HEAD_DIM:(hh + 1) * HEAD_DIM], wri_ref[hh])
        r_parts.append(ri[:, :HEAD_DIM])
        i_parts.append(ri[:, HEAD_DIM:])
    r = jax.nn.sigmoid(jnp.concatenate(r_parts, axis=1) + b_r)
    i_gate = jax.nn.sigmoid(jnp.concatenate(i_parts, axis=1) + b_i)
    log_a = lam_scale * r
    a = jnp.exp(log_a)
    th = jnp.tanh(log_a)
    mult = jnp.sqrt(-2.0 * th / (1.0 - th))
    if is_first_tile_chunk is not None:
        first_row = (lax.broadcasted_iota(jnp.int32, (rows, ch), 0) == 0) & is_first_tile_chunk
        mult = jnp.where(first_row, 1.0, mult)
    u = mult * i_gate * xc
    h, h_last = _linear_scan(a, u, h_prev)
    g_b = _dot(hn, wgb_ref[...])
    y_b = h * (g_b * jax.nn.sigmoid(g_b))

    cv = _dot(hn, wc_ref[...]) * _dot(hn, wv_ref[...])
    conv = (cab + caw[2:3] * cv
            + caw[1:2] * _shift_rows(cv, cv_tail, 1)
            + caw[0:1] * _shift_rows(cv, cv_tail, 2))
    g_a = _dot(hn, wga_ref[...])
    y_a = _dot(hn, wb_ref[...]) * conv * (g_a * jax.nn.sigmoid(g_a))
    return y_a, y_b, (cv[rows - SUBLANES:], xb[rows - SUBLANES:], h_last)


def _mixer_kernel(x_ref, g_ref, wv_ref, wb_ref, wc_ref, wga_ref, wxb_ref, wgb_ref,
                  caw_ref, cab_ref, cbw_ref, cbb_ref, wri_ref, br_ref, bi_ref, lam_ref,
                  ya_ref, yb_ref, cv_tail, xb_tail, h_carry):
    i = pl.program_id(1)
    t_rows = ya_ref.shape[0]

    @pl.when(i == 0)
    def _():
        cv_tail[...] = jnp.zeros_like(cv_tail)
        xb_tail[...] = jnp.zeros_like(xb_tail)
        h_carry[...] = jnp.zeros_like(h_carry)

    neg_lam = -lam_ref[...]
    softplus_neg_lam = jnp.maximum(neg_lam, 0.0) + jnp.log1p(jnp.exp(-jnp.abs(neg_lam)))
    lam_scale = -LRU_C * softplus_neg_lam
    w_refs = (wv_ref, wb_ref, wc_ref, wga_ref, wxb_ref, wgb_ref)
    carries = (cv_tail[...], xb_tail[...], h_carry[...])
    for c in range(t_rows // MIX_ROW_CHUNK):
        rows = pl.ds(c * MIX_ROW_CHUNK, MIX_ROW_CHUNK)
        y_a, y_b, carries = _mixer_chunk(
            x_ref[rows, :], g_ref[...], w_refs, caw_ref[...], cab_ref[...], cbw_ref[...],
            cbb_ref[...], wri_ref, br_ref[...], bi_ref[...], lam_scale, carries,
            (i == 0) if c == 0 else None)
        ya_ref[rows, :] = y_a.astype(ya_ref.dtype)
        yb_ref[rows, :] = y_b.astype(yb_ref.dtype)
    cv_tail[...], xb_tail[...], h_carry[...] = carries


def _attn_kernel(x_ref, ya_ref, yb_ref, woa_ref, wob_ref, gx_ref, wq_ref, k_ref, v_ref,
                 wo_ref, gf_ref, out_ref):
    h = x_ref[...] + _dot(ya_ref[...], woa_ref[...]) + _dot(yb_ref[...], wob_ref[...])
    hn = _rms_norm(h, gx_ref[...]).astype(jnp.bfloat16)
    q = _dot(hn, wq_ref[...]).astype(jnp.bfloat16)
    scale = XATTN_HEAD_DIM ** -0.5
    o_parts = []
    for hd in range(XATTN_HEADS):
        sl = slice(hd * XATTN_HEAD_DIM, (hd + 1) * XATTN_HEAD_DIM)
        s = lax.dot_general(q[:, sl], k_ref[:, sl], (((1,), (1,)), ((), ())),
                            preferred_element_type=jnp.float32) * scale
        p = jnp.exp(s - jnp.max(s, axis=-1, keepdims=True))
        p = p / jnp.sum(p, axis=-1, keepdims=True)
        o_parts.append(_dot(p.astype(jnp.bfloat16), v_ref[:, sl]))
    o = jnp.concatenate(o_parts, axis=1).astype(jnp.bfloat16)
    h2 = h + _dot(o, wo_ref[...])
    out_ref[...] = _rms_norm(h2, gf_ref[...]).astype(out_ref.dtype)


def _resident(shape):
    return pl.BlockSpec(shape, lambda *_: (0,) * len(shape), pipeline_mode=pl.Buffered(1))


def _layer(h, mem, norm_mix_g, w_in, conv_a_w, conv_a_b, conv_b_w, conv_b_b, w_rgate, b_rgate,
           w_igate, b_igate, lru_lambda, w_out, norm_x_g, norm_mem_g, w_q, w_kv, w_o, norm_f_g):
    seq, d = h.shape
    n_mem = mem.shape[0]
    bf16 = jnp.bfloat16
    row = lambda v: v.reshape(1, -1)

    kv = pl.pallas_call(
        _kv_kernel,
        grid=(2 * d // KV_COL_TILE,),
        in_specs=[pl.BlockSpec((n_mem, d), lambda j: (0, 0)),
                  pl.BlockSpec((1, d), lambda j: (0, 0)),
                  pl.BlockSpec((d, KV_COL_TILE), lambda j: (0, j))],
        out_specs=pl.BlockSpec((n_mem, KV_COL_TILE), lambda j: (0, j)),
        out_shape=jax.ShapeDtypeStruct((n_mem, 2 * d), bf16),
        compiler_params=pltpu.CompilerParams(dimension_semantics=("arbitrary",),
                                             vmem_limit_bytes=VMEM_LIMIT_BYTES),
        name="kv_proj",
    )(mem, row(norm_mem_g), w_kv.astype(bf16))

    tm, cb = MIX_SEQ_TILE, MIX_CH_TILE
    n_cb = d // cb
    w_in_bf = w_in.astype(bf16)
    w_ri = jnp.concatenate([w_rgate, w_igate], axis=-1).astype(bf16)
    col = lambda g: pl.BlockSpec((d, cb), lambda j, i, g=g: (0, g * n_cb + j))
    chan = lambda r: pl.BlockSpec((r, cb), lambda j, i: (0, j))
    y_a, y_b = pl.pallas_call(
        _mixer_kernel,
        grid=(n_cb, seq // tm),
        in_specs=[pl.BlockSpec((tm, d), lambda j, i: (i, 0)),
                  pl.BlockSpec((1, d), lambda j, i: (0, 0)),
                  col(0), col(1), col(2), col(3), col(4), col(5),
                  chan(conv_a_w.shape[0]), chan(1), chan(conv_b_w.shape[0]), chan(1),
                  pl.BlockSpec((cb // HEAD_DIM, HEAD_DIM, 2 * HEAD_DIM), lambda j, i: (j, 0, 0)),
                  chan(1), chan(1), chan(1)],
        out_specs=[pl.BlockSpec((tm, cb), lambda j, i: (i, j)),
                   pl.BlockSpec((tm, cb), lambda j, i: (i, j))],
        out_shape=[jax.ShapeDtypeStruct((seq, d), bf16), jax.ShapeDtypeStruct((seq, d), bf16)],
        scratch_shapes=[pltpu.VMEM((SUBLANES, cb), jnp.float32),
                        pltpu.VMEM((SUBLANES, cb), jnp.float32),
                        pltpu.VMEM((SUBLANES, cb), jnp.float32)],
        compiler_params=pltpu.CompilerParams(dimension_semantics=("arbitrary", "arbitrary"),
                                             vmem_limit_bytes=VMEM_LIMIT_BYTES),
        name="mixer",
    )(h, row(norm_mix_g), w_in_bf, w_in_bf, w_in_bf, w_in_bf, w_in_bf, w_in_bf,
      conv_a_w, row(conv_a_b), conv_b_w, row(conv_b_b), w_ri,
      row(b_rgate), row(b_igate), row(lru_lambda))

    ta = ATTN_SEQ_TILE
    w_out_bf = w_out.astype(bf16)
    out = pl.pallas_call(
        _attn_kernel,
        grid=(seq // ta,),
        in_specs=[pl.BlockSpec((ta, d), lambda i: (i, 0)),
                  pl.BlockSpec((ta, d), lambda i: (i, 0)),
                  pl.BlockSpec((ta, d), lambda i: (i, 0)),
                  pl.BlockSpec((d, d), lambda i: (0, 0), pipeline_mode=pl.Buffered(1)),
                  pl.BlockSpec((d, d), lambda i: (1, 0), pipeline_mode=pl.Buffered(1)),
                  _resident((1, d)),
                  _resident((d, d)),
                  pl.BlockSpec((n_mem, d), lambda i: (0, 0), pipeline_mode=pl.Buffered(1)),
                  pl.BlockSpec((n_mem, d), lambda i: (0, 1), pipeline_mode=pl.Buffered(1)),
                  _resident((d, d)),
                  _resident((1, d))],
        out_specs=pl.BlockSpec((ta, d), lambda i: (i, 0)),
        out_shape=jax.ShapeDtypeStruct((seq, d), h.dtype),
        compiler_params=pltpu.CompilerParams(dimension_semantics=("arbitrary",),
                                             vmem_limit_bytes=VMEM_LIMIT_BYTES),
        name="outproj_xattn",
    )(h, y_a, y_b, w_out_bf, w_out_bf, row(norm_x_g), w_q.astype(bf16), kv, kv,
      w_o.astype(bf16), row(norm_f_g))
    return out


def kernel(x, mem, norm_mix_g, w_in, conv_a_w, conv_a_b, conv_b_w, conv_b_b, w_rgate, b_rgate,
           w_igate, b_igate, lru_lambda, w_out, norm_x_g, norm_mem_g, w_q, w_kv, w_o, norm_f_g):
    bsz, seq, d = x.shape
    depth = w_in.shape[0]
    assert bsz == 1 and depth == 1 and d == D_MODEL
    out = _layer(x[0], mem[0], norm_mix_g[0], w_in[0], conv_a_w[0], conv_a_b[0], conv_b_w[0],
                 conv_b_b[0], w_rgate[0], b_rgate[0], w_igate[0], b_igate[0], lru_lambda[0],
                 w_out[0], norm_x_g[0], norm_mem_g[0], w_q[0], w_kv[0], w_o[0], norm_f_g)
    return out[None]
```

```python
import jax
import jax.numpy as jnp
from jax import lax
from jax.experimental import pallas as pl
from jax.experimental.pallas import tpu as pltpu

D_MODEL = 2048
HEAD_DIM = 128
LRU_C = 8.0
XATTN_HEADS = 4
XATTN_HEAD_DIM = D_MODEL // XATTN_HEADS
RMS_EPS = 1e-6

SUBLANES = 8
VMEM_LIMIT_BYTES = 56 * 1024 * 1024

MIX_SEQ_TILE = 512
MIX_ROW_CHUNK = 256
MIX_CH_TILE = 512
ATTN_SEQ_TILE = 256
KV_COL_TILE = 512


def _rms_norm(xf, g):
    ms = jnp.mean(xf * xf, axis=-1, keepdims=True)
    return xf * lax.rsqrt(ms + RMS_EPS) * g


def _dot(a, b):
    return jnp.dot(a, b, preferred_element_type=jnp.float32)


def _kv_kernel(mem_ref, g_ref, wkv_ref, kv_ref):
    mn = _rms_norm(mem_ref[...], g_ref[...]).astype(jnp.bfloat16)
    kv_ref[...] = _dot(mn, wkv_ref[...].astype(jnp.bfloat16)).astype(kv_ref.dtype)


def _group_rows(u):
    return u.reshape(u.shape[0] // SUBLANES, SUBLANES, u.shape[1])


def _shift_rows(u, tail, s):
    rows, ch = u.shape
    rolled = pltpu.roll(_group_rows(u), s, axis=1)
    before = jnp.concatenate([pltpu.roll(tail, s, axis=0)[None], rolled[:-1]], axis=0)
    row = lax.broadcasted_iota(jnp.int32, rolled.shape, 1)
    return jnp.where(row < s, before, rolled).reshape(rows, ch)


def _linear_scan(a, b, h_prev):
    t_rows, ch = a.shape
    a, b = _group_rows(a), _group_rows(b)
    row = lax.broadcasted_iota(jnp.int32, a.shape, 1)
    for s in (1, 2, 4):
        keep = row >= s
        b = a * jnp.where(keep, pltpu.roll(b, s, axis=1), 0.0) + b
        a = a * jnp.where(keep, pltpu.roll(a, s, axis=1), 1.0)
    outs = []
    for g in range(t_rows // SUBLANES):
        h_g = a[g] * h_prev + b[g]
        outs.append(h_g)
        h_prev = jnp.broadcast_to(h_g[SUBLANES - 1:SUBLANES, :], (SUBLANES, ch))
    return jnp.concatenate(outs, axis=0), h_prev


def _mixer_chunk(x, g, w_in, caw, cab, cbw, cbb, w_ri, b_r, b_i, lam_scale, carries,
                 is_first_tile_chunk):
    wv_ref, wb_ref, wc_ref, wga_ref, wxb_ref, wgb_ref = w_in
    cv_tail, xb_tail, h_prev = carries
    rows, ch = x.shape[0], cab.shape[1]
    hn = _rms_norm(x, g).astype(jnp.bfloat16)

    xb = _dot(hn, wxb_ref[...])
    xc = (cbb + cbw[3:4] * xb
          + cbw[2:3] * _shift_rows(xb, xb_tail, 1)
          + cbw[1:2] * _shift_rows(xb, xb_tail, 2)
          + cbw[0:1] * _shift_rows(xb, xb_tail, 3))
    xc_bf = xc.astype(jnp.bfloat16)
    r_parts, i_parts = [], []
    for hh in range(ch // HEAD_DIM):
        ri = _dot(xc_bf[:, hh * HEAD_DIM:(hh + 1) * HEAD_DIM], w_ri[hh])
        r_parts.append(ri[:, :HEAD_DIM])
        i_parts.append(ri[:, HEAD_DIM:])
    r = jax.nn.sigmoid(jnp.concatenate(r_parts, axis=1) + b_r)
    i_gate = jax.nn.sigmoid(jnp.concatenate(i_parts, axis=1) + b_i)
    log_a = lam_scale * r
    a = jnp.exp(log_a)
    th = jnp.tanh(log_a)
    mult = jnp.sqrt(-2.0 * th / (1.0 - th))
    if is_first_tile_chunk is not None:
        first_row = (lax.broadcasted_iota(jnp.int32, (rows, ch), 0) == 0) & is_first_tile_chunk
        mult = jnp.where(first_row, 1.0, mult)
    u = mult * i_gate * xc
    h, h_last = _linear_scan(a, u, h_prev)
    g_b = _dot(hn, wgb_ref[...])
    y_b = h * (g_b * jax.nn.sigmoid(g_b))

    cv = _dot(hn, wc_ref[...]) * _dot(hn, wv_ref[...])
    conv = (cab + caw[2:3] * cv
            + caw[1:2] * _shift_rows(cv, cv_tail, 1)
            + caw[0:1] * _shift_rows(cv, cv_tail, 2))
    g_a = _dot(hn, wga_ref[...])
    y_a = _dot(hn, wb_ref[...]) * conv * (g_a * jax.nn.sigmoid(g_a))
    return y_a, y_b, (cv[rows - SUBLANES:], xb[rows - SUBLANES:], h_last)


def _mixer_kernel(x_ref, g_ref, wv_ref, wb_ref, wc_ref, wga_ref, wxb_ref, wgb_ref,
                  caw_ref, cab_ref, cbw_ref, cbb_ref, wr_ref, wi_ref, br_ref, bi_ref, lam_ref,
                  wout_ref, wq_ref, wo_ref,
                  ya_ref, yb_ref, wout_bf_ref, wq_bf_ref, wo_bf_ref,
                  win_bf, cv_tail, xb_tail, h_carry):
    p, i = pl.program_id(0), pl.program_id(1)
    n_phases = pl.num_programs(0)
    t_rows = ya_ref.shape[0]
    bf16 = jnp.bfloat16

    for src, dst in ((wout_ref, wout_bf_ref), (wq_ref, wq_bf_ref), (wo_ref, wo_bf_ref)):
        dst[...] = src[...].astype(bf16)

    @pl.when(p > 0)
    def _():
        @pl.when(i == 0)
        def _():
            cv_tail[...] = jnp.zeros_like(cv_tail)
            xb_tail[...] = jnp.zeros_like(xb_tail)
            h_carry[...] = jnp.zeros_like(h_carry)

        neg_lam = -lam_ref[...]
        softplus_neg_lam = jnp.maximum(neg_lam, 0.0) + jnp.log1p(jnp.exp(-jnp.abs(neg_lam)))
        lam_scale = -LRU_C * softplus_neg_lam
        w_cur = win_bf.at[(p - 1) % 2]
        w_in = [w_cur.at[g] for g in range(6)]
        w_ri = [jnp.concatenate([wr_ref[hh].astype(bf16), wi_ref[hh].astype(bf16)], axis=1)
                for hh in range(wr_ref.shape[0])]
        carries = (cv_tail[...], xb_tail[...], h_carry[...])
        for c in range(t_rows // MIX_ROW_CHUNK):
            rows = pl.ds(c * MIX_ROW_CHUNK, MIX_ROW_CHUNK)
            y_a, y_b, carries = _mixer_chunk(
                x_ref[rows, :], g_ref[...], w_in, caw_ref[...], cab_ref[...], cbw_ref[...],
                cbb_ref[...], w_ri, br_ref[...], bi_ref[...], lam_scale, carries,
                (i == 0) if c == 0 else None)
            ya_ref[rows, :] = y_a.astype(ya_ref.dtype)
            yb_ref[rows, :] = y_b.astype(yb_ref.dtype)
        cv_tail[...], xb_tail[...], h_carry[...] = carries

    @pl.when(p < n_phases - 1)
    def _():
        piece = wv_ref.shape[0]
        w_next = win_bf.at[p % 2]
        for g, w_ref in enumerate((wv_ref, wb_ref, wc_ref, wga_ref, wxb_ref, wgb_ref)):
            w_next[g, pl.ds(i * piece, piece), :] = w_ref[...].astype(bf16)


def _attn_kernel(x_ref, ya_ref, yb_ref, woa_ref, wob_ref, gx_ref, wq_ref, k_ref, v_ref,
                 wo_ref, gf_ref, out_ref):
    h = x_ref[...] + _dot(ya_ref[...], woa_ref[...]) + _dot(yb_ref[...], wob_ref[...])
    hn = _rms_norm(h, gx_ref[...]).astype(jnp.bfloat16)
    q = _dot(hn, wq_ref[...]).astype(jnp.bfloat16)
    scale = XATTN_HEAD_DIM ** -0.5
    o_parts = []
    for hd in range(XATTN_HEADS):
        sl = slice(hd * XATTN_HEAD_DIM, (hd + 1) * XATTN_HEAD_DIM)
        s = lax.dot_general(q[:, sl], k_ref[:, sl], (((1,), (1,)), ((), ())),
                            preferred_element_type=jnp.float32) * scale
        p = jnp.exp(s - jnp.max(s, axis=-1, keepdims=True))
        p = p / jnp.sum(p, axis=-1, keepdims=True)
        o_parts.append(_dot(p.astype(jnp.bfloat16), v_ref[:, sl]))
    o = jnp.concatenate(o_parts, axis=1).astype(jnp.bfloat16)
    h2 = h + _dot(o, wo_ref[...])
    out_ref[...] = _rms_norm(h2, gf_ref[...]).astype(out_ref.dtype)


def _resident(shape):
    return pl.BlockSpec(shape, lambda *_: (0,) * len(shape), pipeline_mode=pl.Buffered(1))


def _layer(h, mem, norm_mix_g, w_in, conv_a_w, conv_a_b, conv_b_w, conv_b_b, w_rgate, b_rgate,
           w_igate, b_igate, lru_lambda, w_out, norm_x_g, norm_mem_g, w_q, w_kv, w_o, norm_f_g):
    seq, d = h.shape
    n_mem = mem.shape[0]
    bf16 = jnp.bfloat16
    row = lambda v: v.reshape(1, -1)

    kv = pl.pallas_call(
        _kv_kernel,
        grid=(2 * d // KV_COL_TILE,),
        in_specs=[pl.BlockSpec((n_mem, d), lambda j: (0, 0)),
                  pl.BlockSpec((1, d), lambda j: (0, 0)),
                  pl.BlockSpec((d, KV_COL_TILE), lambda j: (0, j))],
        out_specs=pl.BlockSpec((n_mem, KV_COL_TILE), lambda j: (0, j)),
        out_shape=jax.ShapeDtypeStruct((n_mem, 2 * d), bf16),
        compiler_params=pltpu.CompilerParams(dimension_semantics=("arbitrary",),
                                             vmem_limit_bytes=VMEM_LIMIT_BYTES),
        name="kv_proj",
    )(mem, row(norm_mem_g), w_kv)

    tm, cb = MIX_SEQ_TILE, MIX_CH_TILE
    n_cb, n_t = d // cb, seq // tm
    n_steps = n_cb * n_t
    piece = d // n_t
    blk = lambda p: jnp.maximum(p - 1, 0)
    nxt = lambda p: jnp.minimum(p, n_cb - 1)
    step = lambda p, i: jnp.minimum(p * n_t + i, n_steps - 1)
    col = lambda g: pl.BlockSpec((piece, cb), lambda p, i, g=g: (i, g * n_cb + nxt(p)))
    chan = lambda r: pl.BlockSpec((r, cb), lambda p, i: (0, blk(p)))
    gate = pl.BlockSpec((cb // HEAD_DIM, HEAD_DIM, HEAD_DIM), lambda p, i: (blk(p), 0, 0))
    y_spec = pl.BlockSpec((tm, cb), lambda p, i: (jnp.where(p > 0, i, 0), blk(p)))
    rows_of = lambda w: pl.BlockSpec((w.shape[0] // n_steps, d), lambda p, i: (step(p, i), 0))
    y_a, y_b, w_out_bf, w_q_bf, w_o_bf = pl.pallas_call(
        _mixer_kernel,
        grid=(n_cb + 1, n_t),
        in_specs=[pl.BlockSpec((tm, d), lambda p, i: (i, 0)),
                  pl.BlockSpec((1, d), lambda p, i: (0, 0)),
                  col(0), col(1), col(2), col(3), col(4), col(5),
                  chan(conv_a_w.shape[0]), chan(1), chan(conv_b_w.shape[0]), chan(1),
                  gate, gate, chan(1), chan(1), chan(1),
                  rows_of(w_out), rows_of(w_q), rows_of(w_o)],
        out_specs=[y_spec, y_spec, rows_of(w_out), rows_of(w_q), rows_of(w_o)],
        out_shape=[jax.ShapeDtypeStruct((seq, d), bf16), jax.ShapeDtypeStruct((seq, d), bf16),
                   jax.ShapeDtypeStruct(w_out.shape, bf16), jax.ShapeDtypeStruct(w_q.shape, bf16),
                   jax.ShapeDtypeStruct(w_o.shape, bf16)],
        scratch_shapes=[pltpu.VMEM((2, 6, d, cb), bf16),
                        pltpu.VMEM((SUBLANES, cb), jnp.float32),
                        pltpu.VMEM((SUBLANES, cb), jnp.float32),
                        pltpu.VMEM((SUBLANES, cb), jnp.float32)],
        compiler_params=pltpu.CompilerParams(dimension_semantics=("arbitrary", "arbitrary"),
                                             vmem_limit_bytes=VMEM_LIMIT_BYTES),
        name="mixer",
    )(h, row(norm_mix_g), w_in, w_in, w_in, w_in, w_in, w_in,
      conv_a_w, row(conv_a_b), conv_b_w, row(conv_b_b), w_rgate, w_igate,
      row(b_rgate), row(b_igate), row(lru_lambda), w_out, w_q, w_o)

    ta = ATTN_SEQ_TILE
    out = pl.pallas_call(
        _attn_kernel,
        grid=(seq // ta,),
        in_specs=[pl.BlockSpec((ta, d), lambda i: (i, 0)),
                  pl.BlockSpec((ta, d), lambda i: (i, 0)),
                  pl.BlockSpec((ta, d), lambda i: (i, 0)),
                  pl.BlockSpec((d, d), lambda i: (0, 0), pipeline_mode=pl.Buffered(1)),
                  pl.BlockSpec((d, d), lambda i: (1, 0), pipeline_mode=pl.Buffered(1)),
                  _resident((1, d)),
                  _resident((d, d)),
                  pl.BlockSpec((n_mem, d), lambda i: (0, 0), pipeline_mode=pl.Buffered(1)),
                  pl.BlockSpec((n_mem, d), lambda i: (0, 1), pipeline_mode=pl.Buffered(1)),
                  _resident((d, d)),
                  _resident((1, d))],
        out_specs=pl.BlockSpec((ta, d), lambda i: (i, 0)),
        out_shape=jax.ShapeDtypeStruct((seq, d), h.dtype),
        compiler_params=pltpu.CompilerParams(dimension_semantics=("arbitrary",),
                                             vmem_limit_bytes=VMEM_LIMIT_BYTES),
        name="outproj_xattn",
    )(h, y_a, y_b, w_out_bf, w_out_bf, row(norm_x_g), w_q_bf, kv, kv, w_o_bf, row(norm_f_g))
    return out


def kernel(x, mem, norm_mix_g, w_in, conv_a_w, conv_a_b, conv_b_w, conv_b_b, w_rgate, b_rgate,
           w_igate, b_igate, lru_lambda, w_out, norm_x_g, norm_mem_g, w_q, w_kv, w_o, norm_f_g):
    bsz, seq, d = x.shape
    depth = w_in.shape[0]
    assert bsz == 1 and depth == 1 and d == D_MODEL
    out = _layer(x[0], mem[0], norm_mix_g[0], w_in[0], conv_a_w[0], conv_a_b[0], conv_b_w[0],
                 conv_b_b[0], w_rgate[0], b_rgate[0], w_igate[0], b_igate[0], lru_lambda[0],
                 w_out[0], norm_x_g[0], norm_mem_g[0], w_q[0], w_kv[0], w_o[0], norm_f_g)
    return out[None]
```

```python
import jax
import jax.numpy as jnp
from jax import lax
from jax.experimental import pallas as pl
from jax.experimental.pallas import tpu as pltpu

D_MODEL = 2048
HEAD_DIM = 128
LRU_C = 8.0
XATTN_HEADS = 4
XATTN_HEAD_DIM = D_MODEL // XATTN_HEADS
RMS_EPS = 1e-6

SUBLANES = 8
VMEM_LIMIT_BYTES = 56 * 1024 * 1024

MIX_SEQ_TILE = 512
MIX_ROW_CHUNK = 256
MIX_CH_TILE = 512
ATTN_SEQ_TILE = 256
KV_COL_TILE = 512


def _rms_norm(xf, g):
    ms = jnp.mean(xf * xf, axis=-1, keepdims=True)
    return xf * lax.rsqrt(ms + RMS_EPS) * g


def _dot(a, b):
    return jnp.dot(a, b, preferred_element_type=jnp.float32)


def _kv_kernel(mem_ref, g_ref, wkv_ref, kv_ref):
    mn = _rms_norm(mem_ref[...], g_ref[...]).astype(jnp.bfloat16)
    kv_ref[...] = _dot(mn, wkv_ref[...].astype(jnp.bfloat16)).astype(kv_ref.dtype)


def _group_rows(u):
    return u.reshape(u.shape[0] // SUBLANES, SUBLANES, u.shape[1])


def _shift_rows(u, tail, s):
    rows, ch = u.shape
    rolled = pltpu.roll(_group_rows(u), s, axis=1)
    before = jnp.concatenate([pltpu.roll(tail, s, axis=0)[None], rolled[:-1]], axis=0)
    row = lax.broadcasted_iota(jnp.int32, rolled.shape, 1)
    return jnp.where(row < s, before, rolled).reshape(rows, ch)


def _linear_scan(a, b, h_prev):
    t_rows, ch = a.shape
    a, b = _group_rows(a), _group_rows(b)
    row = lax.broadcasted_iota(jnp.int32, a.shape, 1)
    for s in (1, 2, 4):
        keep = row >= s
        b = a * jnp.where(keep, pltpu.roll(b, s, axis=1), 0.0) + b
        a = a * jnp.where(keep, pltpu.roll(a, s, axis=1), 1.0)
    outs = []
    for g in range(t_rows // SUBLANES):
        h_g = a[g] * h_prev + b[g]
        outs.append(h_g)
        h_prev = jnp.broadcast_to(h_g[SUBLANES - 1:SUBLANES, :], (SUBLANES, ch))
    return jnp.concatenate(outs, axis=0), h_prev


def _mixer_chunk(x, g, w_in, caw, cab, cbw, cbb, w_ri, b_r, b_i, lam_scale, carries,
                 is_first_tile_chunk):
    wv_ref, wb_ref, wc_ref, wga_ref, wxb_ref, wgb_ref = w_in
    cv_tail, xb_tail, h_prev = carries
    rows, ch = x.shape[0], cab.shape[1]
    hn = _rms_norm(x, g).astype(jnp.bfloat16)

    xb = _dot(hn, wxb_ref[...])
    xc = (cbb + cbw[3:4] * xb
          + cbw[2:3] * _shift_rows(xb, xb_tail, 1)
          + cbw[1:2] * _shift_rows(xb, xb_tail, 2)
          + cbw[0:1] * _shift_rows(xb, xb_tail, 3))
    xc_bf = xc.astype(jnp.bfloat16)
    r_parts, i_parts = [], []
    for hh in range(ch // HEAD_DIM):
        ri = _dot(xc_bf[:, hh * HEAD_DIM:(hh + 1) * HEAD_DIM], w_ri[hh])
        r_parts.append(ri[:, :HEAD_DIM])
        i_parts.append(ri[:, HEAD_DIM:])
    r = jax.nn.sigmoid(jnp.concatenate(r_parts, axis=1) + b_r)
    i_gate = jax.nn.sigmoid(jnp.concatenate(i_parts, axis=1) + b_i)
    log_a = lam_scale * r
    a = jnp.exp(log_a)
    th = jnp.tanh(log_a)
    mult = jnp.sqrt(-2.0 * th / (1.0 - th))
    if is_first_tile_chunk is not None:
        first_row = (lax.broadcasted_iota(jnp.int32, (rows, ch), 0) == 0) & is_first_tile_chunk
        mult = jnp.where(first_row, 1.0, mult)
    u = mult * i_gate * xc
    h, h_last = _linear_scan(a, u, h_prev)
    g_b = _dot(hn, wgb_ref[...])
    y_b = h * (g_b * jax.nn.sigmoid(g_b))

    cv = _dot(hn, wc_ref[...]) * _dot(hn, wv_ref[...])
    conv = (cab + caw[2:3] * cv
            + caw[1:2] * _shift_rows(cv, cv_tail, 1)
            + caw[0:1] * _shift_rows(cv, cv_tail, 2))
    g_a = _dot(hn, wga_ref[...])
    y_a = _dot(hn, wb_ref[...]) * conv * (g_a * jax.nn.sigmoid(g_a))
    return y_a, y_b, (cv[rows - SUBLANES:], xb[rows - SUBLANES:], h_last)


def _mixer_kernel(x_ref, g_ref, wv_ref, wb_ref, wc_ref, wga_ref, wxb_ref, wgb_ref,
                  caw_ref, cab_ref, cbw_ref, cbb_ref, wr_ref, wi_ref, br_ref, bi_ref, lam_ref,
                  wout_ref, wq_ref, wo_ref,
                  ya_ref, yb_ref, wout_bf_ref, wq_bf_ref, wo_bf_ref,
                  win_bf, cv_tail, xb_tail, h_carry):
    p, i = pl.program_id(0), pl.program_id(1)
    n_phases = pl.num_programs(0)
    t_rows = ya_ref.shape[0]
    bf16 = jnp.bfloat16

    for src, dst in ((wout_ref, wout_bf_ref), (wq_ref, wq_bf_ref), (wo_ref, wo_bf_ref)):
        dst[...] = src[...].astype(bf16)

    @pl.when(p > 0)
    def _():
        @pl.when(i == 0)
        def _():
            cv_tail[...] = jnp.zeros_like(cv_tail)
            xb_tail[...] = jnp.zeros_like(xb_tail)
            h_carry[...] = jnp.zeros_like(h_carry)

        neg_lam = -lam_ref[...]
        softplus_neg_lam = jnp.maximum(neg_lam, 0.0) + jnp.log1p(jnp.exp(-jnp.abs(neg_lam)))
        lam_scale = -LRU_C * softplus_neg_lam
        w_cur = win_bf.at[(p - 1) % 2]
        w_in = [w_cur.at[g] for g in range(6)]
        w_ri = [jnp.concatenate([wr_ref[hh].astype(bf16), wi_ref[hh].astype(bf16)], axis=1)
                for hh in range(wr_ref.shape[0])]
        carries = (cv_tail[...], xb_tail[...], h_carry[...])
        for c in range(t_rows // MIX_ROW_CHUNK):
            rows = pl.ds(c * MIX_ROW_CHUNK, MIX_ROW_CHUNK)
            y_a, y_b, carries = _mixer_chunk(
                x_ref[rows, :], g_ref[...], w_in, caw_ref[...], cab_ref[...], cbw_ref[...],
                cbb_ref[...], w_ri, br_ref[...], bi_ref[...], lam_scale, carries,
                (i == 0) if c == 0 else None)
            ya_ref[rows, :] = y_a.astype(ya_ref.dtype)
            yb_ref[rows, :] = y_b.astype(yb_ref.dtype)
        cv_tail[...], xb_tail[...], h_carry[...] = carries

    @pl.when(p < n_phases - 1)
    def _():
        piece = wv_ref.shape[0]
        w_next = win_bf.at[p % 2]
        for g, w_ref in enumerate((wv_ref, wb_ref, wc_ref, wga_ref, wxb_ref, wgb_ref)):
            w_next[g, pl.ds(i * piece, piece), :] = w_ref[...].astype(bf16)


def _attn_kernel(x_ref, ya_ref, yb_ref, woa_ref, wob_ref, gx_ref, wq_ref, k_ref, v_ref,
                 wo_ref, gf_ref, out_ref):
    h = x_ref[...] + _dot(ya_ref[...], woa_ref[...]) + _dot(yb_ref[...], wob_ref[...])
    hn = _rms_norm(h, gx_ref[...]).astype(jnp.bfloat16)
    q = _dot(hn, wq_ref[...]).astype(jnp.bfloat16)
    scale = XATTN_HEAD_DIM ** -0.5
    o_parts = []
    for hd in range(XATTN_HEADS):
        sl = slice(hd * XATTN_HEAD_DIM, (hd + 1) * XATTN_HEAD_DIM)
        s = lax.dot_general(q[:, sl], k_ref[:, sl], (((1,), (1,)), ((), ())),
                            preferred_element_type=jnp.float32) * scale
        p = jnp.exp(s - jnp.max(s, axis=-1, keepdims=True))
        p = p / jnp.sum(p, axis=-1, keepdims=True)
        o_parts.append(_dot(p.astype(jnp.bfloat16), v_ref[:, sl]))
    o = jnp.concatenate(o_parts, axis=1).astype(jnp.bfloat16)
    h2 = h + _dot(o, wo_ref[...])
    out_ref[...] = _rms_norm(h2, gf_ref[...]).astype(out_ref.dtype)


def _resident(shape):
    return pl.BlockSpec(shape, lambda *_: (0,) * len(shape), pipeline_mode=pl.Buffered(1))


def _layer(h, mem, norm_mix_g, w_in, conv_a_w, conv_a_b, conv_b_w, conv_b_b, w_rgate, b_rgate,
           w_igate, b_igate, lru_lambda, w_out, norm_x_g, norm_mem_g, w_q, w_kv, w_o, norm_f_g):
    seq, d = h.shape
    n_mem = mem.shape[0]
    bf16 = jnp.bfloat16
    row = lambda v: v.reshape(1, -1)

    kv = pl.pallas_call(
        _kv_kernel,
        grid=(2 * d // KV_COL_TILE,),
        in_specs=[pl.BlockSpec((n_mem, d), lambda j: (0, 0)),
                  pl.BlockSpec((1, d), lambda j: (0, 0)),
                  pl.BlockSpec((d, KV_COL_TILE), lambda j: (0, j))],
        out_specs=pl.BlockSpec((n_mem, KV_COL_TILE), lambda j: (0, j)),
        out_shape=jax.ShapeDtypeStruct((n_mem, 2 * d), bf16),
        compiler_params=pltpu.CompilerParams(dimension_semantics=("arbitrary",),
                                             vmem_limit_bytes=VMEM_LIMIT_BYTES),
        name="kv_proj",
    )(mem, row(norm_mem_g), w_kv)

    tm, cb = MIX_SEQ_TILE, MIX_CH_TILE
    n_cb, n_t = d // cb, seq // tm
    n_steps = n_cb * n_t
    piece = d // n_t
    blk = lambda p: jnp.maximum(p - 1, 0)
    nxt = lambda p: jnp.minimum(p, n_cb - 1)
    step = lambda p, i: jnp.minimum(p * n_t + i, n_steps - 1)
    tile = lambda p, i: jnp.where(p > 0, i, 0)
    col = lambda g: pl.BlockSpec((piece, cb), lambda p, i, g=g: (i, g * n_cb + nxt(p)))
    chan = lambda r: pl.BlockSpec((r, cb), lambda p, i: (0, blk(p)))
    gate = pl.BlockSpec((cb // HEAD_DIM, HEAD_DIM, HEAD_DIM), lambda p, i: (blk(p), 0, 0))
    y_spec = pl.BlockSpec((tm, cb), lambda p, i: (tile(p, i), blk(p)))
    rows_of = lambda w: pl.BlockSpec((w.shape[0] // n_steps, d), lambda p, i: (step(p, i), 0))
    y_a, y_b, w_out_bf, w_q_bf, w_o_bf = pl.pallas_call(
        _mixer_kernel,
        grid=(n_cb + 1, n_t),
        in_specs=[pl.BlockSpec((tm, d), lambda p, i: (tile(p, i), 0)),
                  pl.BlockSpec((1, d), lambda p, i: (0, 0)),
                  col(0), col(1), col(2), col(3), col(4), col(5),
                  chan(conv_a_w.shape[0]), chan(1), chan(conv_b_w.shape[0]), chan(1),
                  gate, gate, chan(1), chan(1), chan(1),
                  rows_of(w_out), rows_of(w_q), rows_of(w_o)],
        out_specs=[y_spec, y_spec, rows_of(w_out), rows_of(w_q), rows_of(w_o)],
        out_shape=[jax.ShapeDtypeStruct((seq, d), bf16), jax.ShapeDtypeStruct((seq, d), bf16),
                   jax.ShapeDtypeStruct(w_out.shape, bf16), jax.ShapeDtypeStruct(w_q.shape, bf16),
                   jax.ShapeDtypeStruct(w_o.shape, bf16)],
        scratch_shapes=[pltpu.VMEM((2, 6, d, cb), bf16),
                        pltpu.VMEM((SUBLANES, cb), jnp.float32),
                        pltpu.VMEM((SUBLANES, cb), jnp.float32),
                        pltpu.VMEM((SUBLANES, cb), jnp.float32)],
        compiler_params=pltpu.CompilerParams(dimension_semantics=("arbitrary", "arbitrary"),
                                             vmem_limit_bytes=VMEM_LIMIT_BYTES),
        name="mixer",
    )(h, row(norm_mix_g), w_in, w_in, w_in, w_in, w_in, w_in,
      conv_a_w, row(conv_a_b), conv_b_w, row(conv_b_b), w_rgate, w_igate,
      row(b_rgate), row(b_igate), row(lru_lambda), w_out, w_q, w_o)

    ta = ATTN_SEQ_TILE
    out = pl.pallas_call(
        _attn_kernel,
        grid=(seq // ta,),
        in_specs=[pl.BlockSpec((ta, d), lambda i: (i, 0)),
                  pl.BlockSpec((ta, d), lambda i: (i, 0)),
                  pl.BlockSpec((ta, d), lambda i: (i, 0)),
                  pl.BlockSpec((d, d), lambda i: (0, 0), pipeline_mode=pl.Buffered(1)),
                  pl.BlockSpec((d, d), lambda i: (1, 0), pipeline_mode=pl.Buffered(1)),
                  _resident((1, d)),
                  _resident((d, d)),
                  pl.BlockSpec((n_mem, d), lambda i: (0, 0), pipeline_mode=pl.Buffered(1)),
                  pl.BlockSpec((n_mem, d), lambda i: (0, 1), pipeline_mode=pl.Buffered(1)),
                  _resident((d, d)),
                  _resident((1, d))],
        out_specs=pl.BlockSpec((ta, d), lambda i: (i, 0)),
        out_shape=jax.ShapeDtypeStruct((seq, d), h.dtype),
        compiler_params=pltpu.CompilerParams(dimension_semantics=("arbitrary",),
                                             vmem_limit_bytes=VMEM_LIMIT_BYTES),
        name="outproj_xattn",
    )(h, y_a, y_b, w_out_bf, w_out_bf, row(norm_x_g), w_q_bf, kv, kv, w_o_bf, row(norm_f_g))
    return out


def kernel(x, mem, norm_mix_g, w_in, conv_a_w, conv_a_b, conv_b_w, conv_b_b, w_rgate, b_rgate,
           w_igate, b_igate, lru_lambda, w_out, norm_x_g, norm_mem_g, w_q, w_kv, w_o, norm_f_g):
    bsz, seq, d = x.shape
    depth = w_in.shape[0]
    assert bsz == 1 and depth == 1 and d == D_MODEL
    out = _layer(x[0], mem[0], norm_mix_g[0], w_in[0], conv_a_w[0], conv_a_b[0], conv_b_w[0],
                 conv_b_b[0], w_rgate[0], b_rgate[0], w_igate[0], b_igate[0], lru_lambda[0],
                 w_out[0], norm_x_g[0], norm_mem_g[0], w_q[0], w_kv[0], w_o[0], norm_f_g)
    return out[None]
```

```python
import jax
import jax.numpy as jnp
from jax import lax
from jax.experimental import pallas as pl
from jax.experimental.pallas import tpu as pltpu

D_MODEL = 2048
HEAD_DIM = 128
LRU_C = 8.0
XATTN_HEADS = 4
XATTN_HEAD_DIM = D_MODEL // XATTN_HEADS
RMS_EPS = 1e-6

SUBLANES = 8
VMEM_LIMIT_BYTES = 56 * 1024 * 1024

MIX_SEQ_TILE = 512
MIX_ROW_CHUNK = 256
MIX_SUB_ROWS = 32
_PIECE_ORDER = (4, 5, 2, 0, 1, 3)
_PIECES_PER_SUB = (2, 2, 2, 2, 1, 1, 1, 1)
MIX_CH_TILE = 512
ATTN_SEQ_TILE = 256
KV_COL_TILE = 512


def _rms_norm(xf, g):
    ms = jnp.mean(xf * xf, axis=-1, keepdims=True)
    return xf * lax.rsqrt(ms + RMS_EPS) * g


def _dot(a, b):
    return jnp.dot(a, b, preferred_element_type=jnp.float32)


def _kv_kernel(mem_ref, g_ref, wkv_ref, kv_ref):
    mn = _rms_norm(mem_ref[...], g_ref[...]).astype(jnp.bfloat16)
    kv_ref[...] = _dot(mn, wkv_ref[...].astype(jnp.bfloat16)).astype(kv_ref.dtype)


def _group_rows(u):
    return u.reshape(u.shape[0] // SUBLANES, SUBLANES, u.shape[1])


def _shift_rows(u, tail, s):
    rows, ch = u.shape
    rolled = pltpu.roll(_group_rows(u), s, axis=1)
    before = jnp.concatenate([pltpu.roll(tail, s, axis=0)[None], rolled[:-1]], axis=0)
    row = lax.broadcasted_iota(jnp.int32, rolled.shape, 1)
    return jnp.where(row < s, before, rolled).reshape(rows, ch)


def _linear_scan(a, b, h_prev):
    t_rows, ch = a.shape
    a, b = _group_rows(a), _group_rows(b)
    row = lax.broadcasted_iota(jnp.int32, a.shape, 1)
    for s in (1, 2, 4):
        keep = row >= s
        b = a * jnp.where(keep, pltpu.roll(b, s, axis=1), 0.0) + b
        a = a * jnp.where(keep, pltpu.roll(a, s, axis=1), 1.0)
    outs = []
    for g in range(t_rows // SUBLANES):
        h_g = a[g] * h_prev + b[g]
        outs.append(h_g)
        h_prev = jnp.broadcast_to(h_g[SUBLANES - 1:SUBLANES, :], (SUBLANES, ch))
    return jnp.concatenate(outs, axis=0), h_prev


def _silu(g):
    return g * jax.nn.sigmoid(g)


def _mixer_chunk(proj, caw, cab, cbw, cbb, w_ri, b_r, b_i, lam_scale, carries,
                 is_first_tile_chunk, next_piece, store):
    v_a, b_a, c_a, g_a, xb, g_b = proj
    cv_tail, xb_tail, h_prev = carries
    rows, ch = xb.shape

    xc = (cbb + cbw[3:4] * xb
          + cbw[2:3] * _shift_rows(xb, xb_tail, 1)
          + cbw[1:2] * _shift_rows(xb, xb_tail, 2)
          + cbw[0:1] * _shift_rows(xb, xb_tail, 3))
    xc_bf = xc.astype(jnp.bfloat16)
    r_parts, i_parts = [], []
    for hh in range(ch // HEAD_DIM):
        ri = _dot(xc_bf[:, hh * HEAD_DIM:(hh + 1) * HEAD_DIM], w_ri[hh])
        r_parts.append(ri[:, :HEAD_DIM])
        i_parts.append(ri[:, HEAD_DIM:])
    r_pre = jnp.concatenate(r_parts, axis=1)
    i_pre = jnp.concatenate(i_parts, axis=1)

    for s in range(rows // MIX_SUB_ROWS):
        sl = slice(s * MIX_SUB_ROWS, (s + 1) * MIX_SUB_ROWS)
        tie = next_piece(s)

        r = jax.nn.sigmoid(r_pre[sl] + b_r)
        i_gate = jax.nn.sigmoid(i_pre[sl] + b_i)
        log_a = lam_scale * r
        a = jnp.exp(log_a)
        th = jnp.tanh(log_a)
        mult = jnp.sqrt(-2.0 * th / (1.0 - th))
        if s == 0 and is_first_tile_chunk is not None:
            first_row = ((lax.broadcasted_iota(jnp.int32, (MIX_SUB_ROWS, ch), 0) == 0)
                         & is_first_tile_chunk)
            mult = jnp.where(first_row, 1.0, mult)
        u = mult * i_gate * xc[sl]
        h, h_prev = _linear_scan(a, u, h_prev)
        y_b = h * _silu(g_b[sl])

        cv = c_a[sl] * v_a[sl]
        conv = (cab + caw[2:3] * cv
                + caw[1:2] * _shift_rows(cv, cv_tail, 1)
                + caw[0:1] * _shift_rows(cv, cv_tail, 2))
        cv_tail = cv[MIX_SUB_ROWS - SUBLANES:]
        store(sl, b_a[sl] * conv * _silu(g_a[sl]), y_b)

        if tie is not None:
            h_prev = jnp.where(tie[0], tie[1], h_prev)
    return cv_tail, xb[rows - SUBLANES:], h_prev


def _mixer_kernel(x_ref, g_ref, wv_ref, wb_ref, wc_ref, wga_ref, wxb_ref, wgb_ref,
                  caw_ref, cab_ref, cbw_ref, cbb_ref, wr_ref, wi_ref, br_ref, bi_ref, lam_ref,
                  wout_ref, wq_ref, wo_ref,
                  ya_ref, yb_ref, wout_bf_ref, wq_bf_ref, wo_bf_ref,
                  win_bf, cv_tail, xb_tail, h_carry):
    p, i = pl.program_id(0), pl.program_id(1)
    n_phases = pl.num_programs(0)
    t_rows = ya_ref.shape[0]
    bf16 = jnp.bfloat16

    for src, dst in ((wout_ref, wout_bf_ref), (wq_ref, wq_bf_ref), (wo_ref, wo_bf_ref)):
        dst[...] = src[...].astype(bf16)

    @pl.when(p > 0)
    def _():
        @pl.when(i == 0)
        def _():
            cv_tail[...] = jnp.zeros_like(cv_tail)
            xb_tail[...] = jnp.zeros_like(xb_tail)
            h_carry[...] = jnp.zeros_like(h_carry)

        neg_lam = -lam_ref[...]
        softplus_neg_lam = jnp.maximum(neg_lam, 0.0) + jnp.log1p(jnp.exp(-jnp.abs(neg_lam)))
        lam_scale = -LRU_C * softplus_neg_lam
        w_cur = win_bf.at[(p - 1) % 2]
        w_in = [w_cur.at[g] for g in range(6)]
        w_ri = [jnp.concatenate([wr_ref[hh].astype(bf16), wi_ref[hh].astype(bf16)], axis=1)
                for hh in range(wr_ref.shape[0])]
        carries = (cv_tail[...], xb_tail[...], h_carry[...])
        n_chunks = t_rows // MIX_ROW_CHUNK
        never = p < 0

        def normed(c):
            rows = pl.ds(c * MIX_ROW_CHUNK, MIX_ROW_CHUNK)
            return _rms_norm(x_ref[rows, :], g_ref[...]).astype(bf16)

        hn = normed(0)
        proj = [_dot(hn, w_ref[...]) for w_ref in w_in]
        for c in range(n_chunks):
            hn = normed(c + 1) if c + 1 < n_chunks else None
            halves = [[None, None] for _ in w_in]
            todo = [(k, hf) for k in _PIECE_ORDER for hf in range(2)]

            def next_piece(s, halves=halves, todo=todo, hn=hn):
                if hn is None or not _PIECES_PER_SUB[s]:
                    return None
                tie = None
                for _ in range(_PIECES_PER_SUB[s]):
                    k, hf = todo.pop(0)
                    lanes = pl.ds(hf * (MIX_CH_TILE // 2), MIX_CH_TILE // 2)
                    halves[k][hf] = _dot(hn, w_in[k][:, lanes])
                    last = halves[k][hf][MIX_ROW_CHUNK - SUBLANES:]
                    tie = last if tie is None else tie + last
                return never, jnp.concatenate([tie, tie], axis=1)

            def store(sl, y_a, y_b, c=c):
                rows = pl.ds(c * MIX_ROW_CHUNK + sl.start, sl.stop - sl.start)
                ya_ref[rows, :] = y_a.astype(ya_ref.dtype)
                yb_ref[rows, :] = y_b.astype(yb_ref.dtype)

            carries = _mixer_chunk(
                proj, caw_ref[...], cab_ref[...], cbw_ref[...], cbb_ref[...], w_ri, br_ref[...],
                bi_ref[...], lam_scale, carries, (i == 0) if c == 0 else None, next_piece, store)
            proj = [jnp.concatenate(h2, axis=1) for h2 in halves] if hn is not None else None
        cv_tail[...], xb_tail[...], h_carry[...] = carries

    @pl.when(p < n_phases - 1)
    def _():
        piece = wv_ref.shape[0]
        w_next = win_bf.at[p % 2]
        for g, w_ref in enumerate((wv_ref, wb_ref, wc_ref, wga_ref, wxb_ref, wgb_ref)):
            w_next[g, pl.ds(i * piece, piece), :] = w_ref[...].astype(bf16)


def _attn_kernel(x_ref, ya_ref, yb_ref, woa_ref, wob_ref, gx_ref, wq_ref, k_ref, v_ref,
                 wo_ref, gf_ref, out_ref):
    h = x_ref[...] + _dot(ya_ref[...], woa_ref[...]) + _dot(yb_ref[...], wob_ref[...])
    hn = _rms_norm(h, gx_ref[...]).astype(jnp.bfloat16)
    q = _dot(hn, wq_ref[...]).astype(jnp.bfloat16)
    scale = XATTN_HEAD_DIM ** -0.5
    o_parts = []
    for hd in range(XATTN_HEADS):
        sl = slice(hd * XATTN_HEAD_DIM, (hd + 1) * XATTN_HEAD_DIM)
        s = lax.dot_general(q[:, sl], k_ref[:, sl], (((1,), (1,)), ((), ())),
                            preferred_element_type=jnp.float32) * scale
        p = jnp.exp(s - jnp.max(s, axis=-1, keepdims=True))
        p = p / jnp.sum(p, axis=-1, keepdims=True)
        o_parts.append(_dot(p.astype(jnp.bfloat16), v_ref[:, sl]))
    o = jnp.concatenate(o_parts, axis=1).astype(jnp.bfloat16)
    h2 = h + _dot(o, wo_ref[...])
    out_ref[...] = _rms_norm(h2, gf_ref[...]).astype(out_ref.dtype)


def _resident(shape):
    return pl.BlockSpec(shape, lambda *_: (0,) * len(shape), pipeline_mode=pl.Buffered(1))


def _layer(h, mem, norm_mix_g, w_in, conv_a_w, conv_a_b, conv_b_w, conv_b_b, w_rgate, b_rgate,
           w_igate, b_igate, lru_lambda, w_out, norm_x_g, norm_mem_g, w_q, w_kv, w_o, norm_f_g):
    seq, d = h.shape
    n_mem = mem.shape[0]
    bf16 = jnp.bfloat16
    row = lambda v: v.reshape(1, -1)

    kv = pl.pallas_call(
        _kv_kernel,
        grid=(2 * d // KV_COL_TILE,),
        in_specs=[pl.BlockSpec((n_mem, d), lambda j: (0, 0)),
                  pl.BlockSpec((1, d), lambda j: (0, 0)),
                  pl.BlockSpec((d, KV_COL_TILE), lambda j: (0, j))],
        out_specs=pl.BlockSpec((n_mem, KV_COL_TILE), lambda j: (0, j)),
        out_shape=jax.ShapeDtypeStruct((n_mem, 2 * d), bf16),
        compiler_params=pltpu.CompilerParams(dimension_semantics=("arbitrary",),
                                             vmem_limit_bytes=VMEM_LIMIT_BYTES),
        name="kv_proj",
    )(mem, row(norm_mem_g), w_kv)

    tm, cb = MIX_SEQ_TILE, MIX_CH_TILE
    n_cb, n_t = d // cb, seq // tm
    n_steps = n_cb * n_t
    piece = d // n_t
    blk = lambda p: jnp.maximum(p - 1, 0)
    nxt = lambda p: jnp.minimum(p, n_cb - 1)
    step = lambda p, i: jnp.minimum(p * n_t + i, n_steps - 1)
    tile = lambda p, i: jnp.where(p > 0, i, 0)
    col = lambda g: pl.BlockSpec((piece, cb), lambda p, i, g=g: (i, g * n_cb + nxt(p)))
    chan = lambda r: pl.BlockSpec((r, cb), lambda p, i: (0, blk(p)))
    gate = pl.BlockSpec((cb // HEAD_DIM, HEAD_DIM, HEAD_DIM), lambda p, i: (blk(p), 0, 0))
    y_spec = pl.BlockSpec((tm, cb), lambda p, i: (tile(p, i), blk(p)))
    rows_of = lambda w: pl.BlockSpec((w.shape[0] // n_steps, d), lambda p, i: (step(p, i), 0))
    y_a, y_b, w_out_bf, w_q_bf, w_o_bf = pl.pallas_call(
        _mixer_kernel,
        grid=(n_cb + 1, n_t),
        in_specs=[pl.BlockSpec((tm, d), lambda p, i: (tile(p, i), 0)),
                  pl.BlockSpec((1, d), lambda p, i: (0, 0)),
                  col(0), col(1), col(2), col(3), col(4), col(5),
                  chan(conv_a_w.shape[0]), chan(1), chan(conv_b_w.shape[0]), chan(1),
                  gate, gate, chan(1), chan(1), chan(1),
                  rows_of(w_out), rows_of(w_q), rows_of(w_o)],
        out_specs=[y_spec, y_spec, rows_of(w_out), rows_of(w_q), rows_of(w_o)],
        out_shape=[jax.ShapeDtypeStruct((seq, d), bf16), jax.ShapeDtypeStruct((seq, d), bf16),
                   jax.ShapeDtypeStruct(w_out.shape, bf16), jax.ShapeDtypeStruct(w_q.shape, bf16),
                   jax.ShapeDtypeStruct(w_o.shape, bf16)],
        scratch_shapes=[pltpu.VMEM((2, 6, d, cb), bf16),
                        pltpu.VMEM((SUBLANES, cb), jnp.float32),
                        pltpu.VMEM((SUBLANES, cb), jnp.float32),
                        pltpu.VMEM((SUBLANES, cb), jnp.float32)],
        compiler_params=pltpu.CompilerParams(dimension_semantics=("arbitrary", "arbitrary"),
                                             vmem_limit_bytes=VMEM_LIMIT_BYTES),
        name="mixer",
    )(h, row(norm_mix_g), w_in, w_in, w_in, w_in, w_in, w_in,
      conv_a_w, row(conv_a_b), conv_b_w, row(conv_b_b), w_rgate, w_igate,
      row(b_rgate), row(b_igate), row(lru_lambda), w_out, w_q, w_o)

    ta = ATTN_SEQ_TILE
    out = pl.pallas_call(
        _attn_kernel,
        grid=(seq // ta,),
        in_specs=[pl.BlockSpec((ta, d), lambda i: (i, 0)),
                  pl.BlockSpec((ta, d), lambda i: (i, 0)),
                  pl.BlockSpec((ta, d), lambda i: (i, 0)),
                  pl.BlockSpec((d, d), lambda i: (0, 0), pipeline_mode=pl.Buffered(1)),
                  pl.BlockSpec((d, d), lambda i: (1, 0), pipeline_mode=pl.Buffered(1)),
                  _resident((1, d)),
                  _resident((d, d)),
                  pl.BlockSpec((n_mem, d), lambda i: (0, 0), pipeline_mode=pl.Buffered(1)),
                  pl.BlockSpec((n_mem, d), lambda i: (0, 1), pipeline_mode=pl.Buffered(1)),
                  _resident((d, d)),
                  _resident((1, d))],
        out_specs=pl.BlockSpec((ta, d), lambda i: (i, 0)),
        out_shape=jax.ShapeDtypeStruct((seq, d), h.dtype),
        compiler_params=pltpu.CompilerParams(dimension_semantics=("arbitrary",),
                                             vmem_limit_bytes=VMEM_LIMIT_BYTES),
        name="outproj_xattn",
    )(h, y_a, y_b, w_out_bf, w_out_bf, row(norm_x_g), w_q_bf, kv, kv, w_o_bf, row(norm_f_g))
    return out


def kernel(x, mem, norm_mix_g, w_in, conv_a_w, conv_a_b, conv_b_w, conv_b_b, w_rgate, b_rgate,
           w_igate, b_igate, lru_lambda, w_out, norm_x_g, norm_mem_g, w_q, w_kv, w_o, norm_f_g):
    bsz, seq, d = x.shape
    depth = w_in.shape[0]
    assert bsz == 1 and depth == 1 and d == D_MODEL
    out = _layer(x[0], mem[0], norm_mix_g[0], w_in[0], conv_a_w[0], conv_a_b[0], conv_b_w[0],
                 conv_b_b[0], w_rgate[0], b_rgate[0], w_igate[0], b_igate[0], lru_lambda[0],
                 w_out[0], norm_x_g[0], norm_mem_g[0], w_q[0], w_kv[0], w_o[0], norm_f_g)
    return out[None]
```

```python
import jax
import jax.numpy as jnp
from jax import lax
from jax.experimental import pallas as pl
from jax.experimental.pallas import tpu as pltpu

D_MODEL = 2048
HEAD_DIM = 128
LRU_C = 8.0
XATTN_HEADS = 4
XATTN_HEAD_DIM = D_MODEL // XATTN_HEADS
RMS_EPS = 1e-6

SUBLANES = 8
MXU_COLS = 256
VMEM_LIMIT_BYTES = 56 * 1024 * 1024

MIX_SEQ_TILE = 512
MIX_ROW_CHUNKS = (256, 256)
MIX_SUB_ROWS = 32
_PIECE_ORDER = (4, 5, 2, 0, 1, 3)
MIX_CH_TILE = 512
ATTN_SEQ_TILE = 256
KV_COL_TILE = 512


def _rms_norm(xf, g):
    ms = jnp.mean(xf * xf, axis=-1, keepdims=True)
    return xf * lax.rsqrt(ms + RMS_EPS) * g


def _dot(a, b):
    return jnp.dot(a, b, preferred_element_type=jnp.float32)


def _kv_kernel(mem_ref, g_ref, wkv_ref, kv_ref):
    mn = _rms_norm(mem_ref[...], g_ref[...]).astype(jnp.bfloat16)
    kv_ref[...] = _dot(mn, wkv_ref[...].astype(jnp.bfloat16)).astype(kv_ref.dtype)


def _group_rows(u):
    return u.reshape(u.shape[0] // SUBLANES, SUBLANES, u.shape[1])


def _shift_rows(u, tail, s):
    rows, ch = u.shape
    rolled = pltpu.roll(_group_rows(u), s, axis=1)
    before = jnp.concatenate([pltpu.roll(tail, s, axis=0)[None], rolled[:-1]], axis=0)
    row = lax.broadcasted_iota(jnp.int32, rolled.shape, 1)
    return jnp.where(row < s, before, rolled).reshape(rows, ch)


def _linear_scan(a, b, h_prev):
    t_rows, ch = a.shape
    a, b = _group_rows(a), _group_rows(b)
    row = lax.broadcasted_iota(jnp.int32, a.shape, 1)
    for s in (1, 2, 4):
        keep = row >= s
        b = a * jnp.where(keep, pltpu.roll(b, s, axis=1), 0.0) + b
        a = a * jnp.where(keep, pltpu.roll(a, s, axis=1), 1.0)
    outs = []
    for g in range(t_rows // SUBLANES):
        h_g = a[g] * h_prev + b[g]
        outs.append(h_g)
        h_prev = jnp.broadcast_to(h_g[SUBLANES - 1:SUBLANES, :], (SUBLANES, ch))
    return jnp.concatenate(outs, axis=0), h_prev


def _silu(g):
    return g * jax.nn.sigmoid(g)


def _mixer_chunk(proj, caw, cab, cbw, cbb, w_ri, b_r, b_i, lam_scale, carries,
                 is_first_tile_chunk, next_piece, store):
    v_a, b_a, c_a, g_a, xb, g_b = proj
    cv_tail, xb_tail, h_prev = carries
    rows, ch = xb.shape

    xc = (cbb + cbw[3:4] * xb
          + cbw[2:3] * _shift_rows(xb, xb_tail, 1)
          + cbw[1:2] * _shift_rows(xb, xb_tail, 2)
          + cbw[0:1] * _shift_rows(xb, xb_tail, 3))
    xc_bf = xc.astype(jnp.bfloat16)
    r_parts, i_parts = [], []
    for hh in range(ch // HEAD_DIM):
        ri = _dot(xc_bf[:, hh * HEAD_DIM:(hh + 1) * HEAD_DIM], w_ri[hh])
        r_parts.append(ri[:, :HEAD_DIM])
        i_parts.append(ri[:, HEAD_DIM:])
    r_pre = jnp.concatenate(r_parts, axis=1)
    i_pre = jnp.concatenate(i_parts, axis=1)

    for s in range(rows // MIX_SUB_ROWS):
        sl = slice(s * MIX_SUB_ROWS, (s + 1) * MIX_SUB_ROWS)
        tie = next_piece(s)

        r = jax.nn.sigmoid(r_pre[sl] + b_r)
        i_gate = jax.nn.sigmoid(i_pre[sl] + b_i)
        log_a = lam_scale * r
        a = jnp.exp(log_a)
        th = jnp.tanh(log_a)
        mult = jnp.sqrt(-2.0 * th / (1.0 - th))
        if s == 0 and is_first_tile_chunk is not None:
            first_row = ((lax.broadcasted_iota(jnp.int32, (MIX_SUB_ROWS, ch), 0) == 0)
                         & is_first_tile_chunk)
            mult = jnp.where(first_row, 1.0, mult)
        u = mult * i_gate * xc[sl]
        h, h_prev = _linear_scan(a, u, h_prev)
        y_b = h * _silu(g_b[sl])

        cv = c_a[sl] * v_a[sl]
        conv = (cab + caw[2:3] * cv
                + caw[1:2] * _shift_rows(cv, cv_tail, 1)
                + caw[0:1] * _shift_rows(cv, cv_tail, 2))
        cv_tail = cv[MIX_SUB_ROWS - SUBLANES:]
        store(sl, b_a[sl] * conv * _silu(g_a[sl]), y_b)

        if tie is not None:
            h_prev = jnp.where(tie[0], tie[1], h_prev)
    return cv_tail, xb[rows - SUBLANES:], h_prev


def _mixer_kernel(x_ref, g_ref, wv_ref, wb_ref, wc_ref, wga_ref, wxb_ref, wgb_ref,
                  caw_ref, cab_ref, cbw_ref, cbb_ref, wr_ref, wi_ref, br_ref, bi_ref, lam_ref,
                  wout_ref, wq_ref, wo_ref,
                  ya_ref, yb_ref, wout_bf_ref, wq_bf_ref, wo_bf_ref,
                  win_bf, cv_tail, xb_tail, h_carry):
    p, i = pl.program_id(0), pl.program_id(1)
    n_phases = pl.num_programs(0)
    t_rows = ya_ref.shape[0]
    bf16 = jnp.bfloat16

    for src, dst in ((wout_ref, wout_bf_ref), (wq_ref, wq_bf_ref), (wo_ref, wo_bf_ref)):
        dst[...] = src[...].astype(bf16)

    @pl.when(p > 0)
    def _():
        @pl.when(i == 0)
        def _():
            cv_tail[...] = jnp.zeros_like(cv_tail)
            xb_tail[...] = jnp.zeros_like(xb_tail)
            h_carry[...] = jnp.zeros_like(h_carry)

        neg_lam = -lam_ref[...]
        softplus_neg_lam = jnp.maximum(neg_lam, 0.0) + jnp.log1p(jnp.exp(-jnp.abs(neg_lam)))
        lam_scale = -LRU_C * softplus_neg_lam
        w_cur = win_bf.at[(p - 1) % 2]
        w_in = [w_cur.at[g] for g in range(6)]
        w_ri = [jnp.concatenate([wr_ref[hh].astype(bf16), wi_ref[hh].astype(bf16)], axis=1)
                for hh in range(wr_ref.shape[0])]
        carries = (cv_tail[...], xb_tail[...], h_carry[...])
        assert sum(MIX_ROW_CHUNKS) == t_rows
        starts = [sum(MIX_ROW_CHUNKS[:c]) for c in range(len(MIX_ROW_CHUNKS))]
        n_chunks = len(MIX_ROW_CHUNKS)
        never = p < 0

        def normed(c):
            rows = pl.ds(starts[c], MIX_ROW_CHUNKS[c])
            return _rms_norm(x_ref[rows, :], g_ref[...]).astype(bf16)

        hn = normed(0)
        proj = [_dot(hn, w_ref[...]) for w_ref in w_in]
        for c in range(n_chunks):
            hn = normed(c + 1) if c + 1 < n_chunks else None
            n_split = MIX_CH_TILE // MXU_COLS
            halves = [[None] * n_split for _ in w_in]
            todo = [(k, hf) for k in _PIECE_ORDER for hf in range(n_split)]
            n_sub = MIX_ROW_CHUNKS[c] // MIX_SUB_ROWS
            per_sub = [len(todo) // n_sub + (s < len(todo) % n_sub) for s in range(n_sub)]

            def next_piece(s, halves=halves, todo=todo, hn=hn, per_sub=per_sub, n_split=n_split):
                if hn is None or not per_sub[s]:
                    return None
                tie = None
                for _ in range(per_sub[s]):
                    k, hf = todo.pop(0)
                    lanes = pl.ds(hf * MXU_COLS, MXU_COLS)
                    halves[k][hf] = _dot(hn, w_in[k][:, lanes])
                    last = halves[k][hf][hn.shape[0] - SUBLANES:]
                    tie = last if tie is None else tie + last
                return never, jnp.concatenate([tie] * n_split, axis=1)

            def store(sl, y_a, y_b, c=c):
                rows = pl.ds(starts[c] + sl.start, sl.stop - sl.start)
                ya_ref[rows, :] = y_a.astype(ya_ref.dtype)
                yb_ref[rows, :] = y_b.astype(yb_ref.dtype)

            carries = _mixer_chunk(
                proj, caw_ref[...], cab_ref[...], cbw_ref[...], cbb_ref[...], w_ri, br_ref[...],
                bi_ref[...], lam_scale, carries, (i == 0) if c == 0 else None, next_piece, store)
            proj = [jnp.concatenate(h2, axis=1) for h2 in halves] if hn is not None else None
        cv_tail[...], xb_tail[...], h_carry[...] = carries

    @pl.when(p < n_phases - 1)
    def _():
        piece = wv_ref.shape[0]
        w_next = win_bf.at[p % 2]
        for g, w_ref in enumerate((wv_ref, wb_ref, wc_ref, wga_ref, wxb_ref, wgb_ref)):
            w_next[g, pl.ds(i * piece, piece), :] = w_ref[...].astype(bf16)


def _attn_kernel(x_ref, ya_ref, yb_ref, woa_ref, wob_ref, gx_ref, wq_ref, k_ref, v_ref,
                 wo_ref, gf_ref, out_ref):
    h = x_ref[...] + _dot(ya_ref[...], woa_ref[...]) + _dot(yb_ref[...], wob_ref[...])
    hn = _rms_norm(h, gx_ref[...]).astype(jnp.bfloat16)
    q = _dot(hn, wq_ref[...]).astype(jnp.bfloat16)
    scale = XATTN_HEAD_DIM ** -0.5
    o_parts = []
    for hd in range(XATTN_HEADS):
        sl = slice(hd * XATTN_HEAD_DIM, (hd + 1) * XATTN_HEAD_DIM)
        s = lax.dot_general(q[:, sl], k_ref[:, sl], (((1,), (1,)), ((), ())),
                            preferred_element_type=jnp.float32) * scale
        p = jnp.exp(s - jnp.max(s, axis=-1, keepdims=True))
        p = p / jnp.sum(p, axis=-1, keepdims=True)
        o_parts.append(_dot(p.astype(jnp.bfloat16), v_ref[:, sl]))
    o = jnp.concatenate(o_parts, axis=1).astype(jnp.bfloat16)
    h2 = h + _dot(o, wo_ref[...])
    out_ref[...] = _rms_norm(h2, gf_ref[...]).astype(out_ref.dtype)


def _resident(shape):
    return pl.BlockSpec(shape, lambda *_: (0,) * len(shape), pipeline_mode=pl.Buffered(1))


def _layer(h, mem, norm_mix_g, w_in, conv_a_w, conv_a_b, conv_b_w, conv_b_b, w_rgate, b_rgate,
           w_igate, b_igate, lru_lambda, w_out, norm_x_g, norm_mem_g, w_q, w_kv, w_o, norm_f_g):
    seq, d = h.shape
    n_mem = mem.shape[0]
    bf16 = jnp.bfloat16
    row = lambda v: v.reshape(1, -1)

    kv = pl.pallas_call(
        _kv_kernel,
        grid=(2 * d // KV_COL_TILE,),
        in_specs=[pl.BlockSpec((n_mem, d), lambda j: (0, 0)),
                  pl.BlockSpec((1, d), lambda j: (0, 0)),
                  pl.BlockSpec((d, KV_COL_TILE), lambda j: (0, j))],
        out_specs=pl.BlockSpec((n_mem, KV_COL_TILE), lambda j: (0, j)),
        out_shape=jax.ShapeDtypeStruct((n_mem, 2 * d), bf16),
        compiler_params=pltpu.CompilerParams(dimension_semantics=("arbitrary",),
                                             vmem_limit_bytes=VMEM_LIMIT_BYTES),
        name="kv_proj",
    )(mem, row(norm_mem_g), w_kv)

    tm, cb = MIX_SEQ_TILE, MIX_CH_TILE
    n_cb, n_t = d // cb, seq // tm
    n_steps = n_cb * n_t
    piece = d // n_t
    blk = lambda p: jnp.maximum(p - 1, 0)
    nxt = lambda p: jnp.minimum(p, n_cb - 1)
    step = lambda p, i: jnp.minimum(p * n_t + i, n_steps - 1)
    tile = lambda p, i: jnp.where(p > 0, i, 0)
    col = lambda g: pl.BlockSpec((piece, cb), lambda p, i, g=g: (i, g * n_cb + nxt(p)))
    chan = lambda r: pl.BlockSpec((r, cb), lambda p, i: (0, blk(p)))
    gate = pl.BlockSpec((cb // HEAD_DIM, HEAD_DIM, HEAD_DIM), lambda p, i: (blk(p), 0, 0))
    y_spec = pl.BlockSpec((tm, cb), lambda p, i: (tile(p, i), blk(p)))
    rows_of = lambda w: pl.BlockSpec((w.shape[0] // n_steps, d), lambda p, i: (step(p, i), 0))
    y_a, y_b, w_out_bf, w_q_bf, w_o_bf = pl.pallas_call(
        _mixer_kernel,
        grid=(n_cb + 1, n_t),
        in_specs=[pl.BlockSpec((tm, d), lambda p, i: (tile(p, i), 0)),
                  pl.BlockSpec((1, d), lambda p, i: (0, 0)),
                  col(0), col(1), col(2), col(3), col(4), col(5),
                  chan(conv_a_w.shape[0]), chan(1), chan(conv_b_w.shape[0]), chan(1),
                  gate, gate, chan(1), chan(1), chan(1),
                  rows_of(w_out), rows_of(w_q), rows_of(w_o)],
        out_specs=[y_spec, y_spec, rows_of(w_out), rows_of(w_q), rows_of(w_o)],
        out_shape=[jax.ShapeDtypeStruct((seq, d), bf16), jax.ShapeDtypeStruct((seq, d), bf16),
                   jax.ShapeDtypeStruct(w_out.shape, bf16), jax.ShapeDtypeStruct(w_q.shape, bf16),
                   jax.ShapeDtypeStruct(w_o.shape, bf16)],
        scratch_shapes=[pltpu.VMEM((2, 6, d, cb), bf16),
                        pltpu.VMEM((SUBLANES, cb), jnp.float32),
                        pltpu.VMEM((SUBLANES, cb), jnp.float32),
                        pltpu.VMEM((SUBLANES, cb), jnp.float32)],
        compiler_params=pltpu.CompilerParams(dimension_semantics=("arbitrary", "arbitrary"),
                                             vmem_limit_bytes=VMEM_LIMIT_BYTES),
        name="mixer",
    )(h, row(norm_mix_g), w_in, w_in, w_in, w_in, w_in, w_in,
      conv_a_w, row(conv_a_b), conv_b_w, row(conv_b_b), w_rgate, w_igate,
      row(b_rgate), row(b_igate), row(lru_lambda), w_out, w_q, w_o)

    ta = ATTN_SEQ_TILE
    out = pl.pallas_call(
        _attn_kernel,
        grid=(seq // ta,),
        in_specs=[pl.BlockSpec((ta, d), lambda i: (i, 0)),
                  pl.BlockSpec((ta, d), lambda i: (i, 0)),
                  pl.BlockSpec((ta, d), lambda i: (i, 0)),
                  pl.BlockSpec((d, d), lambda i: (0, 0), pipeline_mode=pl.Buffered(1)),
                  pl.BlockSpec((d, d), lambda i: (1, 0), pipeline_mode=pl.Buffered(1)),
                  _resident((1, d)),
                  _resident((d, d)),
                  pl.BlockSpec((n_mem, d), lambda i: (0, 0), pipeline_mode=pl.Buffered(1)),
                  pl.BlockSpec((n_mem, d), lambda i: (0, 1), pipeline_mode=pl.Buffered(1)),
                  _resident((d, d)),
                  _resident((1, d))],
        out_specs=pl.BlockSpec((ta, d), lambda i: (i, 0)),
        out_shape=jax.ShapeDtypeStruct((seq, d), h.dtype),
        compiler_params=pltpu.CompilerParams(dimension_semantics=("arbitrary",),
                                             vmem_limit_bytes=VMEM_LIMIT_BYTES),
        name="outproj_xattn",
    )(h, y_a, y_b, w_out_bf, w_out_bf, row(norm_x_g), w_q_bf, kv, kv, w_o_bf, row(norm_f_g))
    return out


def kernel(x, mem, norm_mix_g, w_in, conv_a_w, conv_a_b, conv_b_w, conv_b_b, w_rgate, b_rgate,
           w_igate, b_igate, lru_lambda, w_out, norm_x_g, norm_mem_g, w_q, w_kv, w_o, norm_f_g):
    bsz, seq, d = x.shape
    depth = w_in.shape[0]
    assert bsz == 1 and depth == 1 and d == D_MODEL
    out = _layer(x[0], mem[0], norm_mix_g[0], w_in[0], conv_a_w[0], conv_a_b[0], conv_b_w[0],
                 conv_b_b[0], w_rgate[0], b_rgate[0], w_igate[0], b_igate[0], lru_lambda[0],
                 w_out[0], norm_x_g[0], norm_mem_g[0], w_q[0], w_kv[0], w_o[0], norm_f_g)
    return out[None]
```

```python
import jax
import jax.numpy as jnp
from jax import lax
from jax.experimental import pallas as pl
from jax.experimental.pallas import tpu as pltpu

D_MODEL = 2048
HEAD_DIM = 128
LRU_C = 8.0
XATTN_HEADS = 4
XATTN_HEAD_DIM = D_MODEL // XATTN_HEADS
RMS_EPS = 1e-6

SUBLANES = 8
MXU_COLS = 256
VMEM_LIMIT_BYTES = 56 * 1024 * 1024

MIX_SEQ_TILE = 512
MIX_ROW_CHUNKS = (256, 256)
MIX_SUB_ROWS = 32
_XB = 4
_PIECE_ORDER = (_XB, 5, 2, 0, 1, 3)
MIX_CH_TILE = 512
ATTN_SEQ_TILE = 256
KV_COL_TILE = 512


def _rms_norm(xf, g):
    ms = jnp.mean(xf * xf, axis=-1, keepdims=True)
    return xf * lax.rsqrt(ms + RMS_EPS) * g


def _dot(a, b):
    return jnp.dot(a, b, preferred_element_type=jnp.float32)


def _kv_kernel(mem_ref, g_ref, wkv_ref, kv_ref):
    mn = _rms_norm(mem_ref[...], g_ref[...]).astype(jnp.bfloat16)
    kv_ref[...] = _dot(mn, wkv_ref[...].astype(jnp.bfloat16)).astype(kv_ref.dtype)


def _group_rows(u):
    return u.reshape(u.shape[0] // SUBLANES, SUBLANES, u.shape[1])


def _shift_rows(u, tail, s):
    rows, ch = u.shape
    rolled = pltpu.roll(_group_rows(u), s, axis=1)
    before = jnp.concatenate([pltpu.roll(tail, s, axis=0)[None], rolled[:-1]], axis=0)
    row = lax.broadcasted_iota(jnp.int32, rolled.shape, 1)
    return jnp.where(row < s, before, rolled).reshape(rows, ch)


def _linear_scan(a, b, h_prev):
    t_rows, ch = a.shape
    a, b = _group_rows(a), _group_rows(b)
    row = lax.broadcasted_iota(jnp.int32, a.shape, 1)
    for s in (1, 2, 4):
        keep = row >= s
        b = a * jnp.where(keep, pltpu.roll(b, s, axis=1), 0.0) + b
        a = a * jnp.where(keep, pltpu.roll(a, s, axis=1), 1.0)
    outs = []
    for g in range(t_rows // SUBLANES):
        h_g = a[g] * h_prev + b[g]
        outs.append(h_g)
        h_prev = jnp.broadcast_to(h_g[SUBLANES - 1:SUBLANES, :], (SUBLANES, ch))
    return jnp.concatenate(outs, axis=0), h_prev


def _silu(g):
    return g * jax.nn.sigmoid(g)


def _conv_gates(xb, xb_tail, cbw, cbb, w_ri):
    xc = (cbb + cbw[3:4] * xb
          + cbw[2:3] * _shift_rows(xb, xb_tail, 1)
          + cbw[1:2] * _shift_rows(xb, xb_tail, 2)
          + cbw[0:1] * _shift_rows(xb, xb_tail, 3))
    xc_bf = xc.astype(jnp.bfloat16)
    r_parts, i_parts = [], []
    for hh in range(xb.shape[1] // HEAD_DIM):
        ri = _dot(xc_bf[:, hh * HEAD_DIM:(hh + 1) * HEAD_DIM], w_ri[hh])
        r_parts.append(ri[:, :HEAD_DIM])
        i_parts.append(ri[:, HEAD_DIM:])
    return xc, jnp.concatenate(r_parts, axis=1), jnp.concatenate(i_parts, axis=1)


def _mixer_chunk(proj, conv_gates, caw, cab, b_r, b_i, lam_scale, carries,
                 is_first_tile_chunk, next_piece, store):
    v_a, b_a, c_a, g_a, xb, g_b = proj
    xc, r_pre, i_pre = conv_gates
    cv_tail, h_prev = carries
    rows, ch = xb.shape

    for s in range(rows // MIX_SUB_ROWS):
        sl = slice(s * MIX_SUB_ROWS, (s + 1) * MIX_SUB_ROWS)
        tie = next_piece(s)

        r = jax.nn.sigmoid(r_pre[sl] + b_r)
        i_gate = jax.nn.sigmoid(i_pre[sl] + b_i)
        log_a = lam_scale * r
        a = jnp.exp(log_a)
        th = jnp.tanh(log_a)
        mult = jnp.sqrt(-2.0 * th / (1.0 - th))
        if s == 0 and is_first_tile_chunk is not None:
            first_row = ((lax.broadcasted_iota(jnp.int32, (MIX_SUB_ROWS, ch), 0) == 0)
                         & is_first_tile_chunk)
            mult = jnp.where(first_row, 1.0, mult)
        u = mult * i_gate * xc[sl]
        h, h_prev = _linear_scan(a, u, h_prev)
        y_b = h * _silu(g_b[sl])

        cv = c_a[sl] * v_a[sl]
        conv = (cab + caw[2:3] * cv
                + caw[1:2] * _shift_rows(cv, cv_tail, 1)
                + caw[0:1] * _shift_rows(cv, cv_tail, 2))
        cv_tail = cv[MIX_SUB_ROWS - SUBLANES:]
        store(sl, b_a[sl] * conv * _silu(g_a[sl]), y_b)

        if tie is not None:
            h_prev = jnp.where(tie[0], tie[1], h_prev)
    return cv_tail, h_prev


def _mixer_kernel(x_ref, g_ref, wv_ref, wb_ref, wc_ref, wga_ref, wxb_ref, wgb_ref,
                  caw_ref, cab_ref, cbw_ref, cbb_ref, wr_ref, wi_ref, br_ref, bi_ref, lam_ref,
                  wout_ref, wq_ref, wo_ref,
                  ya_ref, yb_ref, wout_bf_ref, wq_bf_ref, wo_bf_ref,
                  win_bf, cv_tail, xb_tail, h_carry):
    p, i = pl.program_id(0), pl.program_id(1)
    n_phases = pl.num_programs(0)
    t_rows = ya_ref.shape[0]
    bf16 = jnp.bfloat16

    for src, dst in ((wout_ref, wout_bf_ref), (wq_ref, wq_bf_ref), (wo_ref, wo_bf_ref)):
        dst[...] = src[...].astype(bf16)

    @pl.when(p > 0)
    def _():
        @pl.when(i == 0)
        def _():
            cv_tail[...] = jnp.zeros_like(cv_tail)
            xb_tail[...] = jnp.zeros_like(xb_tail)
            h_carry[...] = jnp.zeros_like(h_carry)

        neg_lam = -lam_ref[...]
        softplus_neg_lam = jnp.maximum(neg_lam, 0.0) + jnp.log1p(jnp.exp(-jnp.abs(neg_lam)))
        lam_scale = -LRU_C * softplus_neg_lam
        w_cur = win_bf.at[(p - 1) % 2]
        w_in = [w_cur.at[g] for g in range(6)]
        w_ri = [jnp.concatenate([wr_ref[hh].astype(bf16), wi_ref[hh].astype(bf16)], axis=1)
                for hh in range(wr_ref.shape[0])]
        carries = (cv_tail[...], h_carry[...])
        assert sum(MIX_ROW_CHUNKS) == t_rows
        starts = [sum(MIX_ROW_CHUNKS[:c]) for c in range(len(MIX_ROW_CHUNKS))]
        n_chunks = len(MIX_ROW_CHUNKS)
        never = p < 0

        def normed(c):
            rows = pl.ds(starts[c], MIX_ROW_CHUNKS[c])
            return _rms_norm(x_ref[rows, :], g_ref[...]).astype(bf16)

        def conv_gates(xb, tail):
            return _conv_gates(xb, tail, cbw_ref[...], cbb_ref[...], w_ri)

        hn = normed(0)
        proj = [_dot(hn, w_ref[...]) for w_ref in w_in]
        cg = conv_gates(proj[_XB], xb_tail[...])
        for c in range(n_chunks):
            hn = normed(c + 1) if c + 1 < n_chunks else None
            n_split = MIX_CH_TILE // MXU_COLS
            halves = [[None] * n_split for _ in w_in]
            todo = [(k, hf) for k in _PIECE_ORDER for hf in range(n_split)]
            n_sub = MIX_ROW_CHUNKS[c] // MIX_SUB_ROWS
            per_sub = [len(todo) // n_sub + (s < len(todo) % n_sub) for s in range(n_sub)]
            next_cg = []

            def next_piece(s, halves=halves, todo=todo, hn=hn, per_sub=per_sub, n_split=n_split,
                           next_cg=next_cg, xb=proj[_XB]):
                if hn is None:
                    return None
                if s == n_sub // 2:
                    next_xb = jnp.concatenate(halves[_XB], axis=1)
                    next_cg.append(conv_gates(next_xb, xb[xb.shape[0] - SUBLANES:]))
                if not per_sub[s]:
                    return None
                tie = None
                for _ in range(per_sub[s]):
                    k, hf = todo.pop(0)
                    lanes = pl.ds(hf * MXU_COLS, MXU_COLS)
                    halves[k][hf] = _dot(hn, w_in[k][:, lanes])
                    last = halves[k][hf][hn.shape[0] - SUBLANES:]
                    tie = last if tie is None else tie + last
                return never, jnp.concatenate([tie] * n_split, axis=1)

            def store(sl, y_a, y_b, c=c):
                rows = pl.ds(starts[c] + sl.start, sl.stop - sl.start)
                ya_ref[rows, :] = y_a.astype(ya_ref.dtype)
                yb_ref[rows, :] = y_b.astype(yb_ref.dtype)

            carries = _mixer_chunk(
                proj, cg, caw_ref[...], cab_ref[...], br_ref[...], bi_ref[...], lam_scale, carries,
                (i == 0) if c == 0 else None, next_piece, store)
            if hn is None:
                xb_tail[...] = proj[_XB][MIX_ROW_CHUNKS[c] - SUBLANES:]
            else:
                proj, cg = [jnp.concatenate(h2, axis=1) for h2 in halves], next_cg[0]
        cv_tail[...], h_carry[...] = carries

    @pl.when(p < n_phases - 1)
    def _():
        piece = wv_ref.shape[0]
        w_next = win_bf.at[p % 2]
        for g, w_ref in enumerate((wv_ref, wb_ref, wc_ref, wga_ref, wxb_ref, wgb_ref)):
            w_next[g, pl.ds(i * piece, piece), :] = w_ref[...].astype(bf16)


def _attn_kernel(x_ref, ya_ref, yb_ref, woa_ref, wob_ref, gx_ref, wq_ref, k_ref, v_ref,
                 wo_ref, gf_ref, out_ref):
    h = x_ref[...] + _dot(ya_ref[...], woa_ref[...]) + _dot(yb_ref[...], wob_ref[...])
    hn = _rms_norm(h, gx_ref[...]).astype(jnp.bfloat16)
    q = _dot(hn, wq_ref[...]).astype(jnp.bfloat16)
    scale = XATTN_HEAD_DIM ** -0.5
    o_parts = []
    for hd in range(XATTN_HEADS):
        sl = slice(hd * XATTN_HEAD_DIM, (hd + 1) * XATTN_HEAD_DIM)
        s = lax.dot_general(q[:, sl], k_ref[:, sl], (((1,), (1,)), ((), ())),
                            preferred_element_type=jnp.float32) * scale
        p = jnp.exp(s - jnp.max(s, axis=-1, keepdims=True))
        p = p / jnp.sum(p, axis=-1, keepdims=True)
        o_parts.append(_dot(p.astype(jnp.bfloat16), v_ref[:, sl]))
    o = jnp.concatenate(o_parts, axis=1).astype(jnp.bfloat16)
    h2 = h + _dot(o, wo_ref[...])
    out_ref[...] = _rms_norm(h2, gf_ref[...]).astype(out_ref.dtype)


def _resident(shape):
    return pl.BlockSpec(shape, lambda *_: (0,) * len(shape), pipeline_mode=pl.Buffered(1))


def _layer(h, mem, norm_mix_g, w_in, conv_a_w, conv_a_b, conv_b_w, conv_b_b, w_rgate, b_rgate,
           w_igate, b_igate, lru_lambda, w_out, norm_x_g, norm_mem_g, w_q, w_kv, w_o, norm_f_g):
    seq, d = h.shape
    n_mem = mem.shape[0]
    bf16 = jnp.bfloat16
    row = lambda v: v.reshape(1, -1)

    kv = pl.pallas_call(
        _kv_kernel,
        grid=(2 * d // KV_COL_TILE,),
        in_specs=[pl.BlockSpec((n_mem, d), lambda j: (0, 0)),
                  pl.BlockSpec((1, d), lambda j: (0, 0)),
                  pl.BlockSpec((d, KV_COL_TILE), lambda j: (0, j))],
        out_specs=pl.BlockSpec((n_mem, KV_COL_TILE), lambda j: (0, j)),
        out_shape=jax.ShapeDtypeStruct((n_mem, 2 * d), bf16),
        compiler_params=pltpu.CompilerParams(dimension_semantics=("arbitrary",),
                                             vmem_limit_bytes=VMEM_LIMIT_BYTES),
        name="kv_proj",
    )(mem, row(norm_mem_g), w_kv)

    tm, cb = MIX_SEQ_TILE, MIX_CH_TILE
    n_cb, n_t = d // cb, seq // tm
    n_steps = n_cb * n_t
    piece = d // n_t
    blk = lambda p: jnp.maximum(p - 1, 0)
    nxt = lambda p: jnp.minimum(p, n_cb - 1)
    step = lambda p, i: jnp.minimum(p * n_t + i, n_steps - 1)
    tile = lambda p, i: jnp.where(p > 0, i, 0)
    col = lambda g: pl.BlockSpec((piece, cb), lambda p, i, g=g: (i, g * n_cb + nxt(p)))
    chan = lambda r: pl.BlockSpec((r, cb), lambda p, i: (0, blk(p)))
    gate = pl.BlockSpec((cb // HEAD_DIM, HEAD_DIM, HEAD_DIM), lambda p, i: (blk(p), 0, 0))
    y_spec = pl.BlockSpec((tm, cb), lambda p, i: (tile(p, i), blk(p)))
    rows_of = lambda w: pl.BlockSpec((w.shape[0] // n_steps, d), lambda p, i: (step(p, i), 0))
    y_a, y_b, w_out_bf, w_q_bf, w_o_bf = pl.pallas_call(
        _mixer_kernel,
        grid=(n_cb + 1, n_t),
        in_specs=[pl.BlockSpec((tm, d), lambda p, i: (tile(p, i), 0)),
                  pl.BlockSpec((1, d), lambda p, i: (0, 0)),
                  col(0), col(1), col(2), col(3), col(4), col(5),
                  chan(conv_a_w.shape[0]), chan(1), chan(conv_b_w.shape[0]), chan(1),
                  gate, gate, chan(1), chan(1), chan(1),
                  rows_of(w_out), rows_of(w_q), rows_of(w_o)],
        out_specs=[y_spec, y_spec, rows_of(w_out), rows_of(w_q), rows_of(w_o)],
        out_shape=[jax.ShapeDtypeStruct((seq, d), bf16), jax.ShapeDtypeStruct((seq, d), bf16),
                   jax.ShapeDtypeStruct(w_out.shape, bf16), jax.ShapeDtypeStruct(w_q.shape, bf16),
                   jax.ShapeDtypeStruct(w_o.shape, bf16)],
        scratch_shapes=[pltpu.VMEM((2, 6, d, cb), bf16),
                        pltpu.VMEM((SUBLANES, cb), jnp.float32),
                        pltpu.VMEM((SUBLANES, cb), jnp.float32),
                        pltpu.VMEM((SUBLANES, cb), jnp.float32)],
        compiler_params=pltpu.CompilerParams(dimension_semantics=("arbitrary", "arbitrary"),
                                             vmem_limit_bytes=VMEM_LIMIT_BYTES),
        name="mixer",
    )(h, row(norm_mix_g), w_in, w_in, w_in, w_in, w_in, w_in,
      conv_a_w, row(conv_a_b), conv_b_w, row(conv_b_b), w_rgate, w_igate,
      row(b_rgate), row(b_igate), row(lru_lambda), w_out, w_q, w_o)

    ta = ATTN_SEQ_TILE
    out = pl.pallas_call(
        _attn_kernel,
        grid=(seq // ta,),
        in_specs=[pl.BlockSpec((ta, d), lambda i: (i, 0)),
                  pl.BlockSpec((ta, d), lambda i: (i, 0)),
                  pl.BlockSpec((ta, d), lambda i: (i, 0)),
                  pl.BlockSpec((d, d), lambda i: (0, 0), pipeline_mode=pl.Buffered(1)),
                  pl.BlockSpec((d, d), lambda i: (1, 0), pipeline_mode=pl.Buffered(1)),
                  _resident((1, d)),
                  _resident((d, d)),
                  pl.BlockSpec((n_mem, d), lambda i: (0, 0), pipeline_mode=pl.Buffered(1)),
                  pl.BlockSpec((n_mem, d), lambda i: (0, 1), pipeline_mode=pl.Buffered(1)),
                  _resident((d, d)),
                  _resident((1, d))],
        out_specs=pl.BlockSpec((ta, d), lambda i: (i, 0)),
        out_shape=jax.ShapeDtypeStruct((seq, d), h.dtype),
        compiler_params=pltpu.CompilerParams(dimension_semantics=("arbitrary",),
                                             vmem_limit_bytes=VMEM_LIMIT_BYTES),
        name="outproj_xattn",
    )(h, y_a, y_b, w_out_bf, w_out_bf, row(norm_x_g), w_q_bf, kv, kv, w_o_bf, row(norm_f_g))
    return out


def kernel(x, mem, norm_mix_g, w_in, conv_a_w, conv_a_b, conv_b_w, conv_b_b, w_rgate, b_rgate,
           w_igate, b_igate, lru_lambda, w_out, norm_x_g, norm_mem_g, w_q, w_kv, w_o, norm_f_g):
    bsz, seq, d = x.shape
    depth = w_in.shape[0]
    assert bsz == 1 and depth == 1 and d == D_MODEL
    out = _layer(x[0], mem[0], norm_mix_g[0], w_in[0], conv_a_w[0], conv_a_b[0], conv_b_w[0],
                 conv_b_b[0], w_rgate[0], b_rgate[0], w_igate[0], b_igate[0], lru_lambda[0],
                 w_out[0], norm_x_g[0], norm_mem_g[0], w_q[0], w_kv[0], w_o[0], norm_f_g)
    return out[None]
```

```python
import functools

import jax
import jax.numpy as jnp
from jax import lax
from jax.experimental import pallas as pl
from jax.experimental.pallas import tpu as pltpu

D_MODEL = 2048
HEAD_DIM = 128
LRU_C = 8.0
XATTN_HEADS = 4
XATTN_HEAD_DIM = D_MODEL // XATTN_HEADS
RMS_EPS = 1e-6

SUBLANES = 8
MXU_COLS = 256
VMEM_LIMIT_BYTES = 56 * 1024 * 1024

MIX_SEQ_TILE = 512
MIX_ROW_CHUNK = 256
MIX_SUB_ROWS = 32
_XB = 4
_PIECE_ORDER = (_XB, 5, 2, 0, 1, 3)
MIX_CH_TILE = 512
ATTN_SEQ_TILE = 256
KV_COL_TILE = 512


def _rms_norm(xf, g):
    ms = jnp.mean(xf * xf, axis=-1, keepdims=True)
    return xf * lax.rsqrt(ms + RMS_EPS) * g


def _dot(a, b):
    return jnp.dot(a, b, preferred_element_type=jnp.float32)


def _kv_kernel(mem_ref, g_ref, wkv_ref, kv_ref):
    mn = _rms_norm(mem_ref[...], g_ref[...]).astype(jnp.bfloat16)
    kv_ref[...] = _dot(mn, wkv_ref[...].astype(jnp.bfloat16)).astype(kv_ref.dtype)


def _group_rows(u):
    return u.reshape(u.shape[0] // SUBLANES, SUBLANES, u.shape[1])


def _shift_rows(u, tail, s):
    rows, ch = u.shape
    rolled = pltpu.roll(_group_rows(u), s, axis=1)
    before = jnp.concatenate([pltpu.roll(tail, s, axis=0)[None], rolled[:-1]], axis=0)
    row = lax.broadcasted_iota(jnp.int32, rolled.shape, 1)
    return jnp.where(row < s, before, rolled).reshape(rows, ch)


def _linear_scan(a, b, h_prev):
    t_rows, ch = a.shape
    a, b = _group_rows(a), _group_rows(b)
    row = lax.broadcasted_iota(jnp.int32, a.shape, 1)
    for s in (1, 2, 4):
        keep = row >= s
        b = a * jnp.where(keep, pltpu.roll(b, s, axis=1), 0.0) + b
        a = a * jnp.where(keep, pltpu.roll(a, s, axis=1), 1.0)
    outs = []
    for g in range(t_rows // SUBLANES):
        h_g = a[g] * h_prev + b[g]
        outs.append(h_g)
        h_prev = jnp.broadcast_to(h_g[SUBLANES - 1:SUBLANES, :], (SUBLANES, ch))
    return jnp.concatenate(outs, axis=0), h_prev


def _silu(g):
    return g * jax.nn.sigmoid(g)


def _conv_gates(xb, xb_tail, cbw, cbb, w_ri):
    xc = (cbb + cbw[3:4] * xb
          + cbw[2:3] * _shift_rows(xb, xb_tail, 1)
          + cbw[1:2] * _shift_rows(xb, xb_tail, 2)
          + cbw[0:1] * _shift_rows(xb, xb_tail, 3))
    xc_bf = xc.astype(jnp.bfloat16)
    r_parts, i_parts = [], []
    for hh in range(xb.shape[1] // HEAD_DIM):
        ri = _dot(xc_bf[:, hh * HEAD_DIM:(hh + 1) * HEAD_DIM], w_ri[hh])
        r_parts.append(ri[:, :HEAD_DIM])
        i_parts.append(ri[:, HEAD_DIM:])
    return xc, jnp.concatenate(r_parts, axis=1), jnp.concatenate(i_parts, axis=1)


def _mixer_chunk(proj, conv_gates, caw, cab, b_r, b_i, lam_scale, carries,
                 is_first_tile_chunk, next_piece, store):
    v_a, b_a, c_a, g_a, xb, g_b = proj
    xc, r_pre, i_pre = conv_gates
    cv_tail, h_prev = carries
    rows, ch = xb.shape

    for s in range(rows // MIX_SUB_ROWS):
        sl = slice(s * MIX_SUB_ROWS, (s + 1) * MIX_SUB_ROWS)
        tie = next_piece(s)

        r = jax.nn.sigmoid(r_pre[sl] + b_r)
        i_gate = jax.nn.sigmoid(i_pre[sl] + b_i)
        log_a = lam_scale * r
        a = jnp.exp(log_a)
        th = jnp.tanh(log_a)
        mult = jnp.sqrt(-2.0 * th / (1.0 - th))
        if s == 0 and is_first_tile_chunk is not None:
            first_row = ((lax.broadcasted_iota(jnp.int32, (MIX_SUB_ROWS, ch), 0) == 0)
                         & is_first_tile_chunk)
            mult = jnp.where(first_row, 1.0, mult)
        u = mult * i_gate * xc[sl]
        h, h_prev = _linear_scan(a, u, h_prev)
        y_b = h * _silu(g_b[sl])

        cv = c_a[sl] * v_a[sl]
        conv = (cab + caw[2:3] * cv
                + caw[1:2] * _shift_rows(cv, cv_tail, 1)
                + caw[0:1] * _shift_rows(cv, cv_tail, 2))
        cv_tail = cv[MIX_SUB_ROWS - SUBLANES:]
        store(sl, b_a[sl] * conv * _silu(g_a[sl]), y_b)

        if tie is not None:
            h_prev = jnp.where(tie[0], tie[1], h_prev)
    return cv_tail, h_prev


def _mixer_kernel(x_ref, xn_ref, g_ref, wv_ref, wb_ref, wc_ref, wga_ref, wxb_ref, wgb_ref,
                  caw_ref, cab_ref, cbw_ref, cbb_ref, wr_ref, wi_ref, br_ref, bi_ref, lam_ref,
                  wout_ref, wq_ref, wo_ref,
                  ya_ref, yb_ref, wout_bf_ref, wq_bf_ref, wo_bf_ref,
                  win_bf, proj_scr, cv_tail, xb_tail, h_carry, *, n_seq_tiles):
    p, i = pl.program_id(0), pl.program_id(1)
    n_phases = pl.num_programs(0)
    t_rows = ya_ref.shape[0]
    bf16 = jnp.bfloat16

    for src, dst in ((wout_ref, wout_bf_ref), (wq_ref, wq_bf_ref), (wo_ref, wo_bf_ref)):
        dst[...] = src[...].astype(bf16)

    @pl.when(p > 0)
    def _():
        @pl.when(i == 0)
        def _():
            cv_tail[...] = jnp.zeros_like(cv_tail)
            xb_tail[...] = jnp.zeros_like(xb_tail)
            h_carry[...] = jnp.zeros_like(h_carry)

        neg_lam = -lam_ref[...]
        softplus_neg_lam = jnp.maximum(neg_lam, 0.0) + jnp.log1p(jnp.exp(-jnp.abs(neg_lam)))
        lam_scale = -LRU_C * softplus_neg_lam
        w_cur = win_bf.at[(p - 1) % 2]
        w_in = [w_cur.at[g] for g in range(6)]
        w_ri = [jnp.concatenate([wr_ref[hh].astype(bf16), wi_ref[hh].astype(bf16)], axis=1)
                for hh in range(wr_ref.shape[0])]
        chunk = MIX_ROW_CHUNK
        n_chunks = t_rows // chunk
        n_split = MIX_CH_TILE // MXU_COLS
        n_sub = chunk // MIX_SUB_ROWS
        never = p < 0

        def normed(ref, c):
            return _rms_norm(ref[pl.ds(c * chunk, chunk), :], g_ref[...]).astype(bf16)

        @pl.when(i == 0)
        def _():
            hn0 = normed(x_ref, 0)
            for k, w_ref in enumerate(w_in):
                proj_scr[k] = _dot(hn0, w_ref[...])

        def run_chunk(c, proj, tail, carries, hn_next):
            halves = [[None] * n_split for _ in w_in]
            todo = [(k, hf) for k in _PIECE_ORDER for hf in range(n_split)]
            per_sub = [len(todo) // n_sub + (s < len(todo) % n_sub) for s in range(n_sub)]

            def next_piece(s):
                if hn_next is None or not per_sub[s]:
                    return None
                tie = None
                for _ in range(per_sub[s]):
                    k, hf = todo.pop(0)
                    lanes = pl.ds(hf * MXU_COLS, MXU_COLS)
                    halves[k][hf] = _dot(hn_next, w_in[k][:, lanes])
                    last = halves[k][hf][chunk - SUBLANES:]
                    tie = last if tie is None else tie + last
                return never, jnp.concatenate([tie] * n_split, axis=1)

            def store(sl, y_a, y_b):
                rows = pl.ds(c * chunk + sl.start, sl.stop - sl.start)
                ya_ref[rows, :] = y_a.astype(ya_ref.dtype)
                yb_ref[rows, :] = y_b.astype(yb_ref.dtype)

            cg = _conv_gates(proj[_XB], tail, cbw_ref[...], cbb_ref[...], w_ri)
            carries = _mixer_chunk(
                proj, cg, caw_ref[...], cab_ref[...], br_ref[...], bi_ref[...], lam_scale, carries,
                (i == 0) if c == 0 else None, next_piece, store)
            nxt = None if hn_next is None else [jnp.concatenate(h2, axis=1) for h2 in halves]
            return carries, nxt

        proj = [proj_scr[k] for k in range(len(w_in))]
        tail = xb_tail[...]
        carries = (cv_tail[...], h_carry[...])
        for c in range(n_chunks - 1):
            carries, nxt = run_chunk(c, proj, tail, carries, normed(x_ref, c + 1))
            proj, tail = nxt, proj[_XB][chunk - SUBLANES:]

        def finish(carries):
            cv_tail[...], h_carry[...] = carries
            xb_tail[...] = proj[_XB][chunk - SUBLANES:]

        @pl.when(i < n_seq_tiles - 1)
        def _():
            carries2, nxt = run_chunk(n_chunks - 1, proj, tail, carries, normed(xn_ref, 0))
            for k in range(len(w_in)):
                proj_scr[k] = nxt[k]
            finish(carries2)

        @pl.when(i == n_seq_tiles - 1)
        def _():
            finish(run_chunk(n_chunks - 1, proj, tail, carries, None)[0])

    @pl.when(p < n_phases - 1)
    def _():
        piece = wv_ref.shape[0]
        w_next = win_bf.at[p % 2]
        for g, w_ref in enumerate((wv_ref, wb_ref, wc_ref, wga_ref, wxb_ref, wgb_ref)):
            w_next[g, pl.ds(i * piece, piece), :] = w_ref[...].astype(bf16)


def _attn_kernel(x_ref, ya_ref, yb_ref, woa_ref, wob_ref, gx_ref, wq_ref, k_ref, v_ref,
                 wo_ref, gf_ref, out_ref):
    h = x_ref[...] + _dot(ya_ref[...], woa_ref[...]) + _dot(yb_ref[...], wob_ref[...])
    hn = _rms_norm(h, gx_ref[...]).astype(jnp.bfloat16)
    q = _dot(hn, wq_ref[...]).astype(jnp.bfloat16)
    scale = XATTN_HEAD_DIM ** -0.5
    o_parts = []
    for hd in range(XATTN_HEADS):
        sl = slice(hd * XATTN_HEAD_DIM, (hd + 1) * XATTN_HEAD_DIM)
        s = lax.dot_general(q[:, sl], k_ref[:, sl], (((1,), (1,)), ((), ())),
                            preferred_element_type=jnp.float32) * scale
        p = jnp.exp(s - jnp.max(s, axis=-1, keepdims=True))
        p = p / jnp.sum(p, axis=-1, keepdims=True)
        o_parts.append(_dot(p.astype(jnp.bfloat16), v_ref[:, sl]))
    o = jnp.concatenate(o_parts, axis=1).astype(jnp.bfloat16)
    h2 = h + _dot(o, wo_ref[...])
    out_ref[...] = _rms_norm(h2, gf_ref[...]).astype(out_ref.dtype)


def _resident(shape):
    return pl.BlockSpec(shape, lambda *_: (0,) * len(shape), pipeline_mode=pl.Buffered(1))


def _layer(h, mem, norm_mix_g, w_in, conv_a_w, conv_a_b, conv_b_w, conv_b_b, w_rgate, b_rgate,
           w_igate, b_igate, lru_lambda, w_out, norm_x_g, norm_mem_g, w_q, w_kv, w_o, norm_f_g):
    seq, d = h.shape
    n_mem = mem.shape[0]
    bf16 = jnp.bfloat16
    row = lambda v: v.reshape(1, -1)

    kv = pl.pallas_call(
        _kv_kernel,
        grid=(2 * d // KV_COL_TILE,),
        in_specs=[pl.BlockSpec((n_mem, d), lambda j: (0, 0)),
                  pl.BlockSpec((1, d), lambda j: (0, 0)),
                  pl.BlockSpec((d, KV_COL_TILE), lambda j: (0, j))],
        out_specs=pl.BlockSpec((n_mem, KV_COL_TILE), lambda j: (0, j)),
        out_shape=jax.ShapeDtypeStruct((n_mem, 2 * d), bf16),
        compiler_params=pltpu.CompilerParams(dimension_semantics=("arbitrary",),
                                             vmem_limit_bytes=VMEM_LIMIT_BYTES),
        name="kv_proj",
    )(mem, row(norm_mem_g), w_kv)

    tm, cb = MIX_SEQ_TILE, MIX_CH_TILE
    n_cb, n_t = d // cb, seq // tm
    n_steps = n_cb * n_t
    piece = d // n_t
    blk = lambda p: jnp.maximum(p - 1, 0)
    nxt = lambda p: jnp.minimum(p, n_cb - 1)
    step = lambda p, i: jnp.minimum(p * n_t + i, n_steps - 1)
    tile = lambda p, i: jnp.where(p > 0, i, 0)
    col = lambda g: pl.BlockSpec((piece, cb), lambda p, i, g=g: (i, g * n_cb + nxt(p)))
    chan = lambda r: pl.BlockSpec((r, cb), lambda p, i: (0, blk(p)))
    gate = pl.BlockSpec((cb // HEAD_DIM, HEAD_DIM, HEAD_DIM), lambda p, i: (blk(p), 0, 0))
    y_spec = pl.BlockSpec((tm, cb), lambda p, i: (tile(p, i), blk(p)))
    rows_of = lambda w: pl.BlockSpec((w.shape[0] // n_steps, d), lambda p, i: (step(p, i), 0))
    chunks_per_tile = tm // MIX_ROW_CHUNK
    x_next = pl.BlockSpec(
        (MIX_ROW_CHUNK, d),
        lambda p, i: (jnp.minimum((tile(p, i) + 1) * chunks_per_tile, seq // MIX_ROW_CHUNK - 1), 0))
    y_a, y_b, w_out_bf, w_q_bf, w_o_bf = pl.pallas_call(
        functools.partial(_mixer_kernel, n_seq_tiles=n_t),
        grid=(n_cb + 1, n_t),
        in_specs=[pl.BlockSpec((tm, d), lambda p, i: (tile(p, i), 0)),
                  x_next,
                  pl.BlockSpec((1, d), lambda p, i: (0, 0)),
                  col(0), col(1), col(2), col(3), col(4), col(5),
                  chan(conv_a_w.shape[0]), chan(1), chan(conv_b_w.shape[0]), chan(1),
                  gate, gate, chan(1), chan(1), chan(1),
                  rows_of(w_out), rows_of(w_q), rows_of(w_o)],
        out_specs=[y_spec, y_spec, rows_of(w_out), rows_of(w_q), rows_of(w_o)],
        out_shape=[jax.ShapeDtypeStruct((seq, d), bf16), jax.ShapeDtypeStruct((seq, d), bf16),
                   jax.ShapeDtypeStruct(w_out.shape, bf16), jax.ShapeDtypeStruct(w_q.shape, bf16),
                   jax.ShapeDtypeStruct(w_o.shape, bf16)],
        scratch_shapes=[pltpu.VMEM((2, 6, d, cb), bf16),
                        pltpu.VMEM((6, MIX_ROW_CHUNK, cb), jnp.float32),
                        pltpu.VMEM((SUBLANES, cb), jnp.float32),
                        pltpu.VMEM((SUBLANES, cb), jnp.float32),
                        pltpu.VMEM((SUBLANES, cb), jnp.float32)],
        compiler_params=pltpu.CompilerParams(dimension_semantics=("arbitrary", "arbitrary"),
                                             vmem_limit_bytes=VMEM_LIMIT_BYTES),
        name="mixer",
    )(h, h, row(norm_mix_g), w_in, w_in, w_in, w_in, w_in, w_in,
      conv_a_w, row(conv_a_b), conv_b_w, row(conv_b_b), w_rgate, w_igate,
      row(b_rgate), row(b_igate), row(lru_lambda), w_out, w_q, w_o)

    ta = ATTN_SEQ_TILE
    out = pl.pallas_call(
        _attn_kernel,
        grid=(seq // ta,),
        in_specs=[pl.BlockSpec((ta, d), lambda i: (i, 0)),
                  pl.BlockSpec((ta, d), lambda i: (i, 0)),
                  pl.BlockSpec((ta, d), lambda i: (i, 0)),
                  pl.BlockSpec((d, d), lambda i: (0, 0), pipeline_mode=pl.Buffered(1)),
                  pl.BlockSpec((d, d), lambda i: (1, 0), pipeline_mode=pl.Buffered(1)),
                  _resident((1, d)),
                  _resident((d, d)),
                  pl.BlockSpec((n_mem, d), lambda i: (0, 0), pipeline_mode=pl.Buffered(1)),
                  pl.BlockSpec((n_mem, d), lambda i: (0, 1), pipeline_mode=pl.Buffered(1)),
                  _resident((d, d)),
                  _resident((1, d))],
        out_specs=pl.BlockSpec((ta, d), lambda i: (i, 0)),
        out_shape=jax.ShapeDtypeStruct((seq, d), h.dtype),
        compiler_params=pltpu.CompilerParams(dimension_semantics=("arbitrary",),
                                             vmem_limit_bytes=VMEM_LIMIT_BYTES),
        name="outproj_xattn",
    )(h, y_a, y_b, w_out_bf, w_out_bf, row(norm_x_g), w_q_bf, kv, kv, w_o_bf, row(norm_f_g))
    return out


def kernel(x, mem, norm_mix_g, w_in, conv_a_w, conv_a_b, conv_b_w, conv_b_b, w_rgate, b_rgate,
           w_igate, b_igate, lru_lambda, w_out, norm_x_g, norm_mem_g, w_q, w_kv, w_o, norm_f_g):
    bsz, seq, d = x.shape
    depth = w_in.shape[0]
    assert bsz == 1 and depth == 1 and d == D_MODEL
    out = _layer(x[0], mem[0], norm_mix_g[0], w_in[0], conv_a_w[0], conv_a_b[0], conv_b_w[0],
                 conv_b_b[0], w_rgate[0], b_rgate[0], w_igate[0], b_igate[0], lru_lambda[0],
                 w_out[0], norm_x_g[0], norm_mem_g[0], w_q[0], w_kv[0], w_o[0], norm_f_g)
    return out[None]
```

```python
import jax
import jax.numpy as jnp
from jax import lax
from jax.experimental import pallas as pl
from jax.experimental.pallas import tpu as pltpu

D_MODEL = 2048
HEAD_DIM = 128
LRU_C = 8.0
XATTN_HEADS = 4
XATTN_HEAD_DIM = D_MODEL // XATTN_HEADS
RMS_EPS = 1e-6

SUBLANES = 8
MXU_COLS = 256
VMEM_LIMIT_BYTES = 56 * 1024 * 1024

MIX_SEQ_TILE = 512
MIX_ROW_CHUNK = 256
MIX_SUB_ROWS = 32
_V, _B, _C, _GA, _XB, _GB = range(6)
_RECURRENCE_GROUPS = (_XB, _GB)
_CONV_GROUPS = (_C, _V, _B, _GA)
MIX_CH_TILE = 512
ATTN_SEQ_TILE = 256
KV_COL_TILE = 512


def _rms_norm(xf, g):
    ms = jnp.mean(xf * xf, axis=-1, keepdims=True)
    return xf * lax.rsqrt(ms + RMS_EPS) * g


def _dot(a, b):
    return jnp.dot(a, b, preferred_element_type=jnp.float32)


def _kv_kernel(mem_ref, g_ref, wkv_ref, kv_ref):
    mn = _rms_norm(mem_ref[...], g_ref[...]).astype(jnp.bfloat16)
    kv_ref[...] = _dot(mn, wkv_ref[...].astype(jnp.bfloat16)).astype(kv_ref.dtype)


def _group_rows(u):
    return u.reshape(u.shape[0] // SUBLANES, SUBLANES, u.shape[1])


def _shift_rows(u, tail, s):
    rows, ch = u.shape
    rolled = pltpu.roll(_group_rows(u), s, axis=1)
    before = jnp.concatenate([pltpu.roll(tail, s, axis=0)[None], rolled[:-1]], axis=0)
    row = lax.broadcasted_iota(jnp.int32, rolled.shape, 1)
    return jnp.where(row < s, before, rolled).reshape(rows, ch)


def _linear_scan(a, b, h_prev):
    t_rows, ch = a.shape
    a, b = _group_rows(a), _group_rows(b)
    row = lax.broadcasted_iota(jnp.int32, a.shape, 1)
    for s in (1, 2, 4):
        keep = row >= s
        b = a * jnp.where(keep, pltpu.roll(b, s, axis=1), 0.0) + b
        a = a * jnp.where(keep, pltpu.roll(a, s, axis=1), 1.0)
    outs = []
    for g in range(t_rows // SUBLANES):
        h_g = a[g] * h_prev + b[g]
        outs.append(h_g)
        h_prev = jnp.broadcast_to(h_g[SUBLANES - 1:SUBLANES, :], (SUBLANES, ch))
    return jnp.concatenate(outs, axis=0), h_prev


def _silu(g):
    return g * jax.nn.sigmoid(g)


def _conv_gates(xb, xb_tail, cbw, cbb, w_ri):
    xc = (cbb + cbw[3:4] * xb
          + cbw[2:3] * _shift_rows(xb, xb_tail, 1)
          + cbw[1:2] * _shift_rows(xb, xb_tail, 2)
          + cbw[0:1] * _shift_rows(xb, xb_tail, 3))
    xc_bf = xc.astype(jnp.bfloat16)
    r_parts, i_parts = [], []
    for hh in range(xb.shape[1] // HEAD_DIM):
        ri = _dot(xc_bf[:, hh * HEAD_DIM:(hh + 1) * HEAD_DIM], w_ri[hh])
        r_parts.append(ri[:, :HEAD_DIM])
        i_parts.append(ri[:, HEAD_DIM:])
    return xc, jnp.concatenate(r_parts, axis=1), jnp.concatenate(i_parts, axis=1)


def _group_a_piece(v_a, b_a, c_a, g_a, caw, cab, cv_tail):
    cv = c_a * v_a
    conv = (cab + caw[2:3] * cv
            + caw[1:2] * _shift_rows(cv, cv_tail, 1)
            + caw[0:1] * _shift_rows(cv, cv_tail, 2))
    return b_a * conv * _silu(g_a), cv[cv.shape[0] - SUBLANES:]


def _group_b_piece(xc, r_pre, i_pre, g_b, b_r, b_i, lam_scale, h_prev, is_sequence_start):
    r = jax.nn.sigmoid(r_pre + b_r)
    i_gate = jax.nn.sigmoid(i_pre + b_i)
    log_a = lam_scale * r
    a = jnp.exp(log_a)
    th = jnp.tanh(log_a)
    mult = jnp.sqrt(-2.0 * th / (1.0 - th))
    if is_sequence_start is not None:
        first_row = (lax.broadcasted_iota(jnp.int32, mult.shape, 0) == 0) & is_sequence_start
        mult = jnp.where(first_row, 1.0, mult)
    h, h_prev = _linear_scan(a, mult * i_gate * xc, h_prev)
    return h * _silu(g_b), h_prev


def _mixer_kernel(x_ref, g_ref, wv_ref, wb_ref, wc_ref, wga_ref, wxb_ref, wgb_ref,
                  caw_ref, cab_ref, cbw_ref, cbb_ref, wr_ref, wi_ref, br_ref, bi_ref, lam_ref,
                  wout_ref, wq_ref, wo_ref,
                  ya_ref, yb_ref, wout_bf_ref, wq_bf_ref, wo_bf_ref,
                  win_bf, cv_tail, xb_tail, h_carry):
    p, i = pl.program_id(0), pl.program_id(1)
    n_phases = pl.num_programs(0)
    t_rows = ya_ref.shape[0]
    bf16 = jnp.bfloat16

    for src, dst in ((wout_ref, wout_bf_ref), (wq_ref, wq_bf_ref), (wo_ref, wo_bf_ref)):
        dst[...] = src[...].astype(bf16)

    @pl.when(p > 0)
    def _():
        @pl.when(i == 0)
        def _():
            cv_tail[...] = jnp.zeros_like(cv_tail)
            xb_tail[...] = jnp.zeros_like(xb_tail)
            h_carry[...] = jnp.zeros_like(h_carry)

        neg_lam = -lam_ref[...]
        softplus_neg_lam = jnp.maximum(neg_lam, 0.0) + jnp.log1p(jnp.exp(-jnp.abs(neg_lam)))
        lam_scale = -LRU_C * softplus_neg_lam
        w_cur = win_bf.at[(p - 1) % 2]
        w_in = [w_cur.at[g] for g in range(6)]
        w_ri = [jnp.concatenate([wr_ref[hh].astype(bf16), wi_ref[hh].astype(bf16)], axis=1)
                for hh in range(wr_ref.shape[0])]
        chunk = MIX_ROW_CHUNK
        assert t_rows == 2 * chunk
        n_split = MIX_CH_TILE // MXU_COLS
        n_sub = chunk // MIX_SUB_ROWS
        a_rows = chunk // (len(_RECURRENCE_GROUPS) * n_split)
        never = p < 0
        cv_t, xb_t, h_prev = cv_tail[...], xb_tail[...], h_carry[...]

        def normed(c):
            return _rms_norm(x_ref[pl.ds(c * chunk, chunk), :], g_ref[...]).astype(bf16)

        def half(hn, k, hf):
            return _dot(hn, w_in[k][:, pl.ds(hf * MXU_COLS, MXU_COLS)])

        def tied(carry, part):
            rows = part[chunk - SUBLANES:]
            return jnp.where(never, jnp.concatenate([rows] * n_split, axis=1), carry)

        def recurrence(c, hn, xb, g_b, xb_tail_c, h_prev, first):
            xc, r_pre, i_pre = _conv_gates(xb, xb_tail_c, cbw_ref[...], cbb_ref[...], w_ri)
            todo = [(k, hf) for k in _CONV_GROUPS for hf in range(n_split)]
            parts = {k: [None] * n_split for k in _CONV_GROUPS}
            for s in range(n_sub):
                sl = slice(s * MIX_SUB_ROWS, (s + 1) * MIX_SUB_ROWS)
                k, hf = todo.pop(0)
                parts[k][hf] = half(hn, k, hf)
                y_b, h_prev = _group_b_piece(xc[sl], r_pre[sl], i_pre[sl], g_b[sl], br_ref[...],
                                             bi_ref[...], lam_scale, h_prev, first if s == 0 else None)
                yb_ref[pl.ds(c * chunk + sl.start, MIX_SUB_ROWS), :] = y_b.astype(yb_ref.dtype)
                h_prev = tied(h_prev, parts[k][hf])
            assert not todo
            return {k: jnp.concatenate(v, axis=1) for k, v in parts.items()}, h_prev

        def gated_conv(c, proj, cv_t, under):
            for q in range(chunk // a_rows):
                sl = slice(q * a_rows, (q + 1) * a_rows)
                part = under()
                v_a, b_a, c_a, g_a = (proj[k][sl] for k in (_V, _B, _C, _GA))
                y_a, cv_t = _group_a_piece(v_a, b_a, c_a, g_a, caw_ref[...], cab_ref[...], cv_t)
                ya_ref[pl.ds(c * chunk + sl.start, a_rows), :] = y_a.astype(ya_ref.dtype)
                if part is not None:
                    cv_t = tied(cv_t, part)
            return cv_t

        hn0, hn1 = normed(0), normed(1)
        xb0, gb0 = _dot(hn0, w_in[_XB][...]), _dot(hn0, w_in[_GB][...])
        proj0, h_prev = recurrence(0, hn0, xb0, gb0, xb_t, h_prev, i == 0)

        todo1 = [(k, hf) for k in _RECURRENCE_GROUPS for hf in range(n_split)]
        parts1 = {k: [None] * n_split for k in _RECURRENCE_GROUPS}

        def next_recurrence_half():
            k, hf = todo1.pop(0)
            parts1[k][hf] = half(hn1, k, hf)
            return parts1[k][hf]

        cv_t = gated_conv(0, proj0, cv_t, next_recurrence_half)
        assert not todo1
        xb1, gb1 = (jnp.concatenate(parts1[k], axis=1) for k in _RECURRENCE_GROUPS)
        proj1, h_prev = recurrence(1, hn1, xb1, gb1, xb0[chunk - SUBLANES:], h_prev, None)
        cv_t = gated_conv(1, proj1, cv_t, lambda: None)
        cv_tail[...], xb_tail[...], h_carry[...] = cv_t, xb1[chunk - SUBLANES:], h_prev

    @pl.when(p < n_phases - 1)
    def _():
        piece = wv_ref.shape[0]
        w_next = win_bf.at[p % 2]
        for g, w_ref in enumerate((wv_ref, wb_ref, wc_ref, wga_ref, wxb_ref, wgb_ref)):
            w_next[g, pl.ds(i * piece, piece), :] = w_ref[...].astype(bf16)


def _attn_kernel(x_ref, ya_ref, yb_ref, woa_ref, wob_ref, gx_ref, wq_ref, k_ref, v_ref,
                 wo_ref, gf_ref, out_ref):
    h = x_ref[...] + _dot(ya_ref[...], woa_ref[...]) + _dot(yb_ref[...], wob_ref[...])
    hn = _rms_norm(h, gx_ref[...]).astype(jnp.bfloat16)
    q = _dot(hn, wq_ref[...]).astype(jnp.bfloat16)
    scale = XATTN_HEAD_DIM ** -0.5
    o_parts = []
    for hd in range(XATTN_HEADS):
        sl = slice(hd * XATTN_HEAD_DIM, (hd + 1) * XATTN_HEAD_DIM)
        s = lax.dot_general(q[:, sl], k_ref[:, sl], (((1,), (1,)), ((), ())),
                            preferred_element_type=jnp.float32) * scale
        p = jnp.exp(s - jnp.max(s, axis=-1, keepdims=True))
        p = p / jnp.sum(p, axis=-1, keepdims=True)
        o_parts.append(_dot(p.astype(jnp.bfloat16), v_ref[:, sl]))
    o = jnp.concatenate(o_parts, axis=1).astype(jnp.bfloat16)
    h2 = h + _dot(o, wo_ref[...])
    out_ref[...] = _rms_norm(h2, gf_ref[...]).astype(out_ref.dtype)


def _resident(shape):
    return pl.BlockSpec(shape, lambda *_: (0,) * len(shape), pipeline_mode=pl.Buffered(1))


def _layer(h, mem, norm_mix_g, w_in, conv_a_w, conv_a_b, conv_b_w, conv_b_b, w_rgate, b_rgate,
           w_igate, b_igate, lru_lambda, w_out, norm_x_g, norm_mem_g, w_q, w_kv, w_o, norm_f_g):
    seq, d = h.shape
    n_mem = mem.shape[0]
    bf16 = jnp.bfloat16
    row = lambda v: v.reshape(1, -1)

    kv = pl.pallas_call(
        _kv_kernel,
        grid=(2 * d // KV_COL_TILE,),
        in_specs=[pl.BlockSpec((n_mem, d), lambda j: (0, 0)),
                  pl.BlockSpec((1, d), lambda j: (0, 0)),
                  pl.BlockSpec((d, KV_COL_TILE), lambda j: (0, j))],
        out_specs=pl.BlockSpec((n_mem, KV_COL_TILE), lambda j: (0, j)),
        out_shape=jax.ShapeDtypeStruct((n_mem, 2 * d), bf16),
        compiler_params=pltpu.CompilerParams(dimension_semantics=("arbitrary",),
                                             vmem_limit_bytes=VMEM_LIMIT_BYTES),
        name="kv_proj",
    )(mem, row(norm_mem_g), w_kv)

    tm, cb = MIX_SEQ_TILE, MIX_CH_TILE
    n_cb, n_t = d // cb, seq // tm
    n_steps = n_cb * n_t
    piece = d // n_t
    blk = lambda p: jnp.maximum(p - 1, 0)
    nxt = lambda p: jnp.minimum(p, n_cb - 1)
    step = lambda p, i: jnp.minimum(p * n_t + i, n_steps - 1)
    tile = lambda p, i: jnp.where(p > 0, i, 0)
    col = lambda g: pl.BlockSpec((piece, cb), lambda p, i, g=g: (i, g * n_cb + nxt(p)))
    chan = lambda r: pl.BlockSpec((r, cb), lambda p, i: (0, blk(p)))
    gate = pl.BlockSpec((cb // HEAD_DIM, HEAD_DIM, HEAD_DIM), lambda p, i: (blk(p), 0, 0))
    y_spec = pl.BlockSpec((tm, cb), lambda p, i: (tile(p, i), blk(p)))
    rows_of = lambda w: pl.BlockSpec((w.shape[0] // n_steps, d), lambda p, i: (step(p, i), 0))
    y_a, y_b, w_out_bf, w_q_bf, w_o_bf = pl.pallas_call(
        _mixer_kernel,
        grid=(n_cb + 1, n_t),
        in_specs=[pl.BlockSpec((tm, d), lambda p, i: (tile(p, i), 0)),
                  pl.BlockSpec((1, d), lambda p, i: (0, 0)),
                  col(0), col(1), col(2), col(3), col(4), col(5),
                  chan(conv_a_w.shape[0]), chan(1), chan(conv_b_w.shape[0]), chan(1),
                  gate, gate, chan(1), chan(1), chan(1),
                  rows_of(w_out), rows_of(w_q), rows_of(w_o)],
        out_specs=[y_spec, y_spec, rows_of(w_out), rows_of(w_q), rows_of(w_o)],
        out_shape=[jax.ShapeDtypeStruct((seq, d), bf16), jax.ShapeDtypeStruct((seq, d), bf16),
                   jax.ShapeDtypeStruct(w_out.shape, bf16), jax.ShapeDtypeStruct(w_q.shape, bf16),
                   jax.ShapeDtypeStruct(w_o.shape, bf16)],
        scratch_shapes=[pltpu.VMEM((2, 6, d, cb), bf16),
                        pltpu.VMEM((SUBLANES, cb), jnp.float32),
                        pltpu.VMEM((SUBLANES, cb), jnp.float32),
                        pltpu.VMEM((SUBLANES, cb), jnp.float32)],
        compiler_params=pltpu.CompilerParams(dimension_semantics=("arbitrary", "arbitrary"),
                                             vmem_limit_bytes=VMEM_LIMIT_BYTES),
        name="mixer",
    )(h, row(norm_mix_g), w_in, w_in, w_in, w_in, w_in, w_in,
      conv_a_w, row(conv_a_b), conv_b_w, row(conv_b_b), w_rgate, w_igate,
      row(b_rgate), row(b_igate), row(lru_lambda), w_out, w_q, w_o)

    ta = ATTN_SEQ_TILE
    out = pl.pallas_call(
        _attn_kernel,
        grid=(seq // ta,),
        in_specs=[pl.BlockSpec((ta, d), lambda i: (i, 0)),
                  pl.BlockSpec((ta, d), lambda i: (i, 0)),
                  pl.BlockSpec((ta, d), lambda i: (i, 0)),
                  pl.BlockSpec((d, d), lambda i: (0, 0), pipeline_mode=pl.Buffered(1)),
                  pl.BlockSpec((d, d), lambda i: (1, 0), pipeline_mode=pl.Buffered(1)),
                  _resident((1, d)),
                  _resident((d, d)),
                  pl.BlockSpec((n_mem, d), lambda i: (0, 0), pipeline_mode=pl.Buffered(1)),
                  pl.BlockSpec((n_mem, d), lambda i: (0, 1), pipeline_mode=pl.Buffered(1)),
                  _resident((d, d)),
                  _resident((1, d))],
        out_specs=pl.BlockSpec((ta, d), lambda i: (i, 0)),
        out_shape=jax.ShapeDtypeStruct((seq, d), h.dtype),
        compiler_params=pltpu.CompilerParams(dimension_semantics=("arbitrary",),
                                             vmem_limit_bytes=VMEM_LIMIT_BYTES),
        name="outproj_xattn",
    )(h, y_a, y_b, w_out_bf, w_out_bf, row(norm_x_g), w_q_bf, kv, kv, w_o_bf, row(norm_f_g))
    return out


def kernel(x, mem, norm_mix_g, w_in, conv_a_w, conv_a_b, conv_b_w, conv_b_b, w_rgate, b_rgate,
           w_igate, b_igate, lru_lambda, w_out, norm_x_g, norm_mem_g, w_q, w_kv, w_o, norm_f_g):
    bsz, seq, d = x.shape
    depth = w_in.shape[0]
    assert bsz == 1 and depth == 1 and d == D_MODEL
    out = _layer(x[0], mem[0], norm_mix_g[0], w_in[0], conv_a_w[0], conv_a_b[0], conv_b_w[0],
                 conv_b_b[0], w_rgate[0], b_rgate[0], w_igate[0], b_igate[0], lru_lambda[0],
                 w_out[0], norm_x_g[0], norm_mem_g[0], w_q[0], w_kv[0], w_o[0], norm_f_g)
    return out[None]
```

```python
import jax
import jax.numpy as jnp
from jax import lax
from jax.experimental import pallas as pl
from jax.experimental.pallas import tpu as pltpu

D_MODEL = 2048
HEAD_DIM = 128
LRU_C = 8.0
XATTN_HEADS = 4
XATTN_HEAD_DIM = D_MODEL // XATTN_HEADS
RMS_EPS = 1e-6

SUBLANES = 8
MXU_COLS = 256
VMEM_LIMIT_BYTES = 56 * 1024 * 1024

MIX_SEQ_TILE = 512
MIX_ROW_CHUNK = 256
MIX_SUB_ROWS = 32
_V, _B, _C, _GA, _XB, _GB = range(6)
_RECURRENCE_GROUPS = (_XB, _GB)
_CONV_GROUPS = (_C, _V, _B, _GA)
MIX_CH_TILE = 512
ATTN_SEQ_TILE = 256
KV_COL_TILE = 512


def _rms_norm(xf, g):
    ms = jnp.mean(xf * xf, axis=-1, keepdims=True)
    return xf * lax.rsqrt(ms + RMS_EPS) * g


def _dot(a, b):
    return jnp.dot(a, b, preferred_element_type=jnp.float32)


def _kv_kernel(mem_ref, g_ref, wkv_ref, kv_ref):
    mn = _rms_norm(mem_ref[...], g_ref[...]).astype(jnp.bfloat16)
    kv_ref[...] = _dot(mn, wkv_ref[...].astype(jnp.bfloat16)).astype(kv_ref.dtype)


def _group_rows(u):
    return u.reshape(u.shape[0] // SUBLANES, SUBLANES, u.shape[1])


def _shift_rows(u, tail, s):
    rows, ch = u.shape
    rolled = pltpu.roll(_group_rows(u), s, axis=1)
    before = jnp.concatenate([pltpu.roll(tail, s, axis=0)[None], rolled[:-1]], axis=0)
    row = lax.broadcasted_iota(jnp.int32, rolled.shape, 1)
    return jnp.where(row < s, before, rolled).reshape(rows, ch)


def _linear_scan(a, b, h_prev):
    t_rows, ch = a.shape
    a, b = _group_rows(a), _group_rows(b)
    row = lax.broadcasted_iota(jnp.int32, a.shape, 1)
    for s in (1, 2, 4):
        keep = row >= s
        b = a * jnp.where(keep, pltpu.roll(b, s, axis=1), 0.0) + b
        a = a * jnp.where(keep, pltpu.roll(a, s, axis=1), 1.0)
    outs = []
    for g in range(t_rows // SUBLANES):
        h_g = a[g] * h_prev + b[g]
        outs.append(h_g)
        h_prev = jnp.broadcast_to(h_g[SUBLANES - 1:SUBLANES, :], (SUBLANES, ch))
    return jnp.concatenate(outs, axis=0), h_prev


def _silu(g):
    return g * jax.nn.sigmoid(g)


def _conv_gates(xb, xb_tail, cbw, cbb, w_ri):
    xc = (cbb + cbw[3:4] * xb
          + cbw[2:3] * _shift_rows(xb, xb_tail, 1)
          + cbw[1:2] * _shift_rows(xb, xb_tail, 2)
          + cbw[0:1] * _shift_rows(xb, xb_tail, 3))
    xc_bf = xc.astype(jnp.bfloat16)
    r_parts, i_parts = [], []
    for hh in range(xb.shape[1] // HEAD_DIM):
        ri = _dot(xc_bf[:, hh * HEAD_DIM:(hh + 1) * HEAD_DIM], w_ri[hh])
        r_parts.append(ri[:, :HEAD_DIM])
        i_parts.append(ri[:, HEAD_DIM:])
    return xc, jnp.concatenate(r_parts, axis=1), jnp.concatenate(i_parts, axis=1)


def _group_a_piece(v_a, b_a, c_a, g_a, caw, cab, cv_tail):
    cv = c_a * v_a
    conv = (cab + caw[2:3] * cv
            + caw[1:2] * _shift_rows(cv, cv_tail, 1)
            + caw[0:1] * _shift_rows(cv, cv_tail, 2))
    return b_a * conv * _silu(g_a), cv[cv.shape[0] - SUBLANES:]


def _group_b_piece(xc, r_pre, i_pre, g_b, b_r, b_i, lam_scale, h_prev, is_sequence_start):
    r = jax.nn.sigmoid(r_pre + b_r)
    i_gate = jax.nn.sigmoid(i_pre + b_i)
    log_a = lam_scale * r
    a = jnp.exp(log_a)
    th = jnp.tanh(log_a)
    mult = jnp.sqrt(-2.0 * th / (1.0 - th))
    if is_sequence_start is not None:
        first_row = (lax.broadcasted_iota(jnp.int32, mult.shape, 0) == 0) & is_sequence_start
        mult = jnp.where(first_row, 1.0, mult)
    h, h_prev = _linear_scan(a, mult * i_gate * xc, h_prev)
    return h * _silu(g_b), h_prev


def _mixer_kernel(x_ref, g_ref, wv_ref, wb_ref, wc_ref, wga_ref, wxb_ref, wgb_ref,
                  caw_ref, cab_ref, cbw_ref, cbb_ref, wr_ref, wi_ref, br_ref, bi_ref, lam_ref,
                  wout_ref, wq_ref, wo_ref,
                  ya_ref, yb_ref, wout_bf_ref, wq_bf_ref, wo_bf_ref,
                  win_bf, cv_tail, xb_tail, h_carry):
    p, i = pl.program_id(0), pl.program_id(1)
    n_phases = pl.num_programs(0)
    t_rows = ya_ref.shape[0]
    bf16 = jnp.bfloat16

    for src, dst in ((wout_ref, wout_bf_ref), (wq_ref, wq_bf_ref), (wo_ref, wo_bf_ref)):
        dst[...] = src[...].astype(bf16)

    @pl.when(p > 0)
    def _():
        @pl.when(i == 0)
        def _():
            cv_tail[...] = jnp.zeros_like(cv_tail)
            xb_tail[...] = jnp.zeros_like(xb_tail)
            h_carry[...] = jnp.zeros_like(h_carry)

        neg_lam = -lam_ref[...]
        softplus_neg_lam = jnp.maximum(neg_lam, 0.0) + jnp.log1p(jnp.exp(-jnp.abs(neg_lam)))
        lam_scale = -LRU_C * softplus_neg_lam
        w_cur = win_bf.at[(p - 1) % 2]
        w_in = [w_cur.at[g] for g in range(6)]
        w_ri = [jnp.concatenate([wr_ref[hh].astype(bf16), wi_ref[hh].astype(bf16)], axis=1)
                for hh in range(wr_ref.shape[0])]
        chunk = MIX_ROW_CHUNK
        assert t_rows == 2 * chunk
        n_split = MIX_CH_TILE // MXU_COLS
        n_sub = chunk // MIX_SUB_ROWS
        a_rows = chunk // (len(_RECURRENCE_GROUPS) * n_split)
        never = p < 0
        cv_t, xb_t, h_prev = cv_tail[...], xb_tail[...], h_carry[...]

        def normed(c):
            return _rms_norm(x_ref[pl.ds(c * chunk, chunk), :], g_ref[...]).astype(bf16)

        def half(hn, k, hf):
            return _dot(hn, w_in[k][:, pl.ds(hf * MXU_COLS, MXU_COLS)])

        def tied(carry, part):
            rows = part[chunk - SUBLANES:]
            return jnp.where(never, jnp.concatenate([rows] * n_split, axis=1), carry)

        def gated_conv(c, parts, hf, cv_half, pieces, under):
            lanes = slice(hf * MXU_COLS, (hf + 1) * MXU_COLS)
            for q in pieces:
                sl = slice(q * a_rows, (q + 1) * a_rows)
                part = under()
                v_a, b_a, c_a, g_a = (parts[k][hf][sl] for k in (_V, _B, _C, _GA))
                y_a, cv_half = _group_a_piece(v_a, b_a, c_a, g_a, caw_ref[:, lanes],
                                              cab_ref[:, lanes], cv_half)
                ya_ref[pl.ds(c * chunk + sl.start, a_rows), lanes] = y_a.astype(ya_ref.dtype)
                if part is not None:
                    cv_half = jnp.where(never, part[chunk - SUBLANES:], cv_half)
            return cv_half

        def recurrence(c, hn, xb, g_b, xb_tail_c, h_prev, cv_half, first):
            xc, r_pre, i_pre = _conv_gates(xb, xb_tail_c, cbw_ref[...], cbb_ref[...], w_ri)
            todo = [(k, hf) for hf in range(n_split) for k in _CONV_GROUPS]
            parts = {k: [None] * n_split for k in _CONV_GROUPS}
            n_first = len(_CONV_GROUPS)
            for s in range(n_sub):
                sl = slice(s * MIX_SUB_ROWS, (s + 1) * MIX_SUB_ROWS)
                k, hf = todo.pop(0)
                parts[k][hf] = half(hn, k, hf)
                y_b, h_prev = _group_b_piece(xc[sl], r_pre[sl], i_pre[sl], g_b[sl], br_ref[...],
                                             bi_ref[...], lam_scale, h_prev, first if s == 0 else None)
                yb_ref[pl.ds(c * chunk + sl.start, MIX_SUB_ROWS), :] = y_b.astype(yb_ref.dtype)
                if s >= n_first:
                    share = (chunk // a_rows) // (n_sub - n_first)
                    done = (s - n_first) * share
                    cv_half = gated_conv(c, parts, 0, cv_half, range(done, done + share), lambda: None)
                h_prev = tied(h_prev, parts[k][hf])
            assert not todo
            return parts, h_prev, cv_half

        assert n_split == 2
        cv_halves = [cv_t[:, :MXU_COLS], cv_t[:, MXU_COLS:]]
        hn0, hn1 = normed(0), normed(1)
        xb0, gb0 = _dot(hn0, w_in[_XB][...]), _dot(hn0, w_in[_GB][...])
        parts0, h_prev, cv_halves[0] = recurrence(0, hn0, xb0, gb0, xb_t, h_prev, cv_halves[0], i == 0)

        todo1 = [(k, hf) for k in _RECURRENCE_GROUPS for hf in range(n_split)]
        parts1 = {k: [None] * n_split for k in _RECURRENCE_GROUPS}

        def next_recurrence_half():
            k, hf = todo1.pop(0)
            parts1[k][hf] = half(hn1, k, hf)
            return parts1[k][hf]

        all_pieces = range(chunk // a_rows)
        cv_halves[1] = gated_conv(0, parts0, 1, cv_halves[1], all_pieces, next_recurrence_half)
        assert not todo1
        xb1, gb1 = (jnp.concatenate(parts1[k], axis=1) for k in _RECURRENCE_GROUPS)
        parts1a, h_prev, cv_halves[0] = recurrence(1, hn1, xb1, gb1, xb0[chunk - SUBLANES:], h_prev,
                                                   cv_halves[0], None)
        cv_halves[1] = gated_conv(1, parts1a, 1, cv_halves[1], all_pieces, lambda: None)
        cv_tail[...] = jnp.concatenate(cv_halves, axis=1)
        xb_tail[...], h_carry[...] = xb1[chunk - SUBLANES:], h_prev

    @pl.when(p < n_phases - 1)
    def _():
        piece = wv_ref.shape[0]
        w_next = win_bf.at[p % 2]
        for g, w_ref in enumerate((wv_ref, wb_ref, wc_ref, wga_ref, wxb_ref, wgb_ref)):
            w_next[g, pl.ds(i * piece, piece), :] = w_ref[...].astype(bf16)


def _attn_kernel(x_ref, ya_ref, yb_ref, woa_ref, wob_ref, gx_ref, wq_ref, k_ref, v_ref,
                 wo_ref, gf_ref, out_ref):
    h = x_ref[...] + _dot(ya_ref[...], woa_ref[...]) + _dot(yb_ref[...], wob_ref[...])
    hn = _rms_norm(h, gx_ref[...]).astype(jnp.bfloat16)
    q = _dot(hn, wq_ref[...]).astype(jnp.bfloat16)
    scale = XATTN_HEAD_DIM ** -0.5
    o_parts = []
    for hd in range(XATTN_HEADS):
        sl = slice(hd * XATTN_HEAD_DIM, (hd + 1) * XATTN_HEAD_DIM)
        s = lax.dot_general(q[:, sl], k_ref[:, sl], (((1,), (1,)), ((), ())),
                            preferred_element_type=jnp.float32) * scale
        p = jnp.exp(s - jnp.max(s, axis=-1, keepdims=True))
        p = p / jnp.sum(p, axis=-1, keepdims=True)
        o_parts.append(_dot(p.astype(jnp.bfloat16), v_ref[:, sl]))
    o = jnp.concatenate(o_parts, axis=1).astype(jnp.bfloat16)
    h2 = h + _dot(o, wo_ref[...])
    out_ref[...] = _rms_norm(h2, gf_ref[...]).astype(out_ref.dtype)


def _resident(shape):
    return pl.BlockSpec(shape, lambda *_: (0,) * len(shape), pipeline_mode=pl.Buffered(1))


def _layer(h, mem, norm_mix_g, w_in, conv_a_w, conv_a_b, conv_b_w, conv_b_b, w_rgate, b_rgate,
           w_igate, b_igate, lru_lambda, w_out, norm_x_g, norm_mem_g, w_q, w_kv, w_o, norm_f_g):
    seq, d = h.shape
    n_mem = mem.shape[0]
    bf16 = jnp.bfloat16
    row = lambda v: v.reshape(1, -1)

    kv = pl.pallas_call(
        _kv_kernel,
        grid=(2 * d // KV_COL_TILE,),
        in_specs=[pl.BlockSpec((n_mem, d), lambda j: (0, 0)),
                  pl.BlockSpec((1, d), lambda j: (0, 0)),
                  pl.BlockSpec((d, KV_COL_TILE), lambda j: (0, j))],
        out_specs=pl.BlockSpec((n_mem, KV_COL_TILE), lambda j: (0, j)),
        out_shape=jax.ShapeDtypeStruct((n_mem, 2 * d), bf16),
        compiler_params=pltpu.CompilerParams(dimension_semantics=("arbitrary",),
                                             vmem_limit_bytes=VMEM_LIMIT_BYTES),
        name="kv_proj",
    )(mem, row(norm_mem_g), w_kv)

    tm, cb = MIX_SEQ_TILE, MIX_CH_TILE
    n_cb, n_t = d // cb, seq // tm
    n_steps = n_cb * n_t
    piece = d // n_t
    blk = lambda p: jnp.maximum(p - 1, 0)
    nxt = lambda p: jnp.minimum(p, n_cb - 1)
    step = lambda p, i: jnp.minimum(p * n_t + i, n_steps - 1)
    tile = lambda p, i: jnp.where(p > 0, i, 0)
    col = lambda g: pl.BlockSpec((piece, cb), lambda p, i, g=g: (i, g * n_cb + nxt(p)))
    chan = lambda r: pl.BlockSpec((r, cb), lambda p, i: (0, blk(p)))
    gate = pl.BlockSpec((cb // HEAD_DIM, HEAD_DIM, HEAD_DIM), lambda p, i: (blk(p), 0, 0))
    y_spec = pl.BlockSpec((tm, cb), lambda p, i: (tile(p, i), blk(p)))
    rows_of = lambda w: pl.BlockSpec((w.shape[0] // n_steps, d), lambda p, i: (step(p, i), 0))
    y_a, y_b, w_out_bf, w_q_bf, w_o_bf = pl.pallas_call(
        _mixer_kernel,
        grid=(n_cb + 1, n_t),
        in_specs=[pl.BlockSpec((tm, d), lambda p, i: (tile(p, i), 0)),
                  pl.BlockSpec((1, d), lambda p, i: (0, 0)),
                  col(0), col(1), col(2), col(3), col(4), col(5),
                  chan(conv_a_w.shape[0]), chan(1), chan(conv_b_w.shape[0]), chan(1),
                  gate, gate, chan(1), chan(1), chan(1),
                  rows_of(w_out), rows_of(w_q), rows_of(w_o)],
        out_specs=[y_spec, y_spec, rows_of(w_out), rows_of(w_q), rows_of(w_o)],
        out_shape=[jax.ShapeDtypeStruct((seq, d), bf16), jax.ShapeDtypeStruct((seq, d), bf16),
                   jax.ShapeDtypeStruct(w_out.shape, bf16), jax.ShapeDtypeStruct(w_q.shape, bf16),
                   jax.ShapeDtypeStruct(w_o.shape, bf16)],
        scratch_shapes=[pltpu.VMEM((2, 6, d, cb), bf16),
                        pltpu.VMEM((SUBLANES, cb), jnp.float32),
                        pltpu.VMEM((SUBLANES, cb), jnp.float32),
                        pltpu.VMEM((SUBLANES, cb), jnp.float32)],
        compiler_params=pltpu.CompilerParams(dimension_semantics=("arbitrary", "arbitrary"),
                                             vmem_limit_bytes=VMEM_LIMIT_BYTES),
        name="mixer",
    )(h, row(norm_mix_g), w_in, w_in, w_in, w_in, w_in, w_in,
      conv_a_w, row(conv_a_b), conv_b_w, row(conv_b_b), w_rgate, w_igate,
      row(b_rgate), row(b_igate), row(lru_lambda), w_out, w_q, w_o)

    ta = ATTN_SEQ_TILE
    out = pl.pallas_call(
        _attn_kernel,
        grid=(seq // ta,),
        in_specs=[pl.BlockSpec((ta, d), lambda i: (i, 0)),
                  pl.BlockSpec((ta, d), lambda i: (i, 0)),
                  pl.BlockSpec((ta, d), lambda i: (i, 0)),
                  pl.BlockSpec((d, d), lambda i: (0, 0), pipeline_mode=pl.Buffered(1)),
                  pl.BlockSpec((d, d), lambda i: (1, 0), pipeline_mode=pl.Buffered(1)),
                  _resident((1, d)),
                  _resident((d, d)),
                  pl.BlockSpec((n_mem, d), lambda i: (0, 0), pipeline_mode=pl.Buffered(1)),
                  pl.BlockSpec((n_mem, d), lambda i: (0, 1), pipeline_mode=pl.Buffered(1)),
                  _resident((d, d)),
                  _resident((1, d))],
        out_specs=pl.BlockSpec((ta, d), lambda i: (i, 0)),
        out_shape=jax.ShapeDtypeStruct((seq, d), h.dtype),
        compiler_params=pltpu.CompilerParams(dimension_semantics=("arbitrary",),
                                             vmem_limit_bytes=VMEM_LIMIT_BYTES),
        name="outproj_xattn",
    )(h, y_a, y_b, w_out_bf, w_out_bf, row(norm_x_g), w_q_bf, kv, kv, w_o_bf, row(norm_f_g))
    return out


def kernel(x, mem, norm_mix_g, w_in, conv_a_w, conv_a_b, conv_b_w, conv_b_b, w_rgate, b_rgate,
           w_igate, b_igate, lru_lambda, w_out, norm_x_g, norm_mem_g, w_q, w_kv, w_o, norm_f_g):
    bsz, seq, d = x.shape
    depth = w_in.shape[0]
    assert bsz == 1 and depth == 1 and d == D_MODEL
    out = _layer(x[0], mem[0], norm_mix_g[0], w_in[0], conv_a_w[0], conv_a_b[0], conv_b_w[0],
                 conv_b_b[0], w_rgate[0], b_rgate[0], w_igate[0], b_igate[0], lru_lambda[0],
                 w_out[0], norm_x_g[0], norm_mem_g[0], w_q[0], w_kv[0], w_o[0], norm_f_g)
    return out[None]
```

```python
import jax
import jax.numpy as jnp
from jax import lax
from jax.experimental import pallas as pl
from jax.experimental.pallas import tpu as pltpu

D_MODEL = 2048
HEAD_DIM = 128
LRU_C = 8.0
XATTN_HEADS = 4
XATTN_HEAD_DIM = D_MODEL // XATTN_HEADS
RMS_EPS = 1e-6

SUBLANES = 8
MXU_COLS = 256
VMEM_LIMIT_BYTES = 56 * 1024 * 1024

MIX_SEQ_TILE = 512
MIX_ROW_CHUNK = 256
MIX_SUB_ROWS = 32
_V, _B, _C, _GA, _XB, _GB = range(6)
_RECURRENCE_GROUPS = (_XB, _GB)
_CONV_GROUPS = (_C, _V, _B, _GA)
MIX_GATE_ROWS = 256
MIX_CH_TILE = 512
ATTN_SEQ_TILE = 256
KV_COL_TILE = 512


def _rms_norm(xf, g):
    ms = jnp.mean(xf * xf, axis=-1, keepdims=True)
    return xf * lax.rsqrt(ms + RMS_EPS) * g


def _dot(a, b):
    return jnp.dot(a, b, preferred_element_type=jnp.float32)


def _kv_kernel(mem_ref, g_ref, wkv_ref, kv_ref):
    mn = _rms_norm(mem_ref[...], g_ref[...]).astype(jnp.bfloat16)
    kv_ref[...] = _dot(mn, wkv_ref[...].astype(jnp.bfloat16)).astype(kv_ref.dtype)


def _group_rows(u):
    return u.reshape(u.shape[0] // SUBLANES, SUBLANES, u.shape[1])


def _shift_rows(u, tail, s):
    rows, ch = u.shape
    rolled = pltpu.roll(_group_rows(u), s, axis=1)
    before = jnp.concatenate([pltpu.roll(tail, s, axis=0)[None], rolled[:-1]], axis=0)
    row = lax.broadcasted_iota(jnp.int32, rolled.shape, 1)
    return jnp.where(row < s, before, rolled).reshape(rows, ch)


def _linear_scan(a, b, h_prev):
    t_rows, ch = a.shape
    a, b = _group_rows(a), _group_rows(b)
    row = lax.broadcasted_iota(jnp.int32, a.shape, 1)
    for s in (1, 2, 4):
        keep = row >= s
        b = a * jnp.where(keep, pltpu.roll(b, s, axis=1), 0.0) + b
        a = a * jnp.where(keep, pltpu.roll(a, s, axis=1), 1.0)
    outs = []
    for g in range(t_rows // SUBLANES):
        h_g = a[g] * h_prev + b[g]
        outs.append(h_g)
        h_prev = jnp.broadcast_to(h_g[SUBLANES - 1:SUBLANES, :], (SUBLANES, ch))
    return jnp.concatenate(outs, axis=0), h_prev


def _silu(g):
    return g * jax.nn.sigmoid(g)


def _conv4(xb, xb_tail, cbw, cbb):
    return (cbb + cbw[3:4] * xb
            + cbw[2:3] * _shift_rows(xb, xb_tail, 1)
            + cbw[1:2] * _shift_rows(xb, xb_tail, 2)
            + cbw[0:1] * _shift_rows(xb, xb_tail, 3))


def _gates(xc, w_ri):
    xc_bf = xc.astype(jnp.bfloat16)
    r_parts, i_parts = [], []
    for hh in range(xc.shape[1] // HEAD_DIM):
        ri = _dot(xc_bf[:, hh * HEAD_DIM:(hh + 1) * HEAD_DIM], w_ri[hh])
        r_parts.append(ri[:, :HEAD_DIM])
        i_parts.append(ri[:, HEAD_DIM:])
    return jnp.concatenate(r_parts, axis=1), jnp.concatenate(i_parts, axis=1)


def _group_a_piece(v_a, b_a, c_a, g_a, caw, cab, cv_tail):
    cv = c_a * v_a
    conv = (cab + caw[2:3] * cv
            + caw[1:2] * _shift_rows(cv, cv_tail, 1)
            + caw[0:1] * _shift_rows(cv, cv_tail, 2))
    return b_a * conv * _silu(g_a), cv[cv.shape[0] - SUBLANES:]


def _group_b_piece(xc, r_pre, i_pre, g_b, b_r, b_i, lam_scale, h_prev, is_sequence_start):
    r = jax.nn.sigmoid(r_pre + b_r)
    i_gate = jax.nn.sigmoid(i_pre + b_i)
    log_a = lam_scale * r
    a = jnp.exp(log_a)
    th = jnp.tanh(log_a)
    mult = jnp.sqrt(-2.0 * th / (1.0 - th))
    if is_sequence_start is not None:
        first_row = (lax.broadcasted_iota(jnp.int32, mult.shape, 0) == 0) & is_sequence_start
        mult = jnp.where(first_row, 1.0, mult)
    h, h_prev = _linear_scan(a, mult * i_gate * xc, h_prev)
    return h * _silu(g_b), h_prev


def _mixer_kernel(x_ref, g_ref, wv_ref, wb_ref, wc_ref, wga_ref, wxb_ref, wgb_ref,
                  caw_ref, cab_ref, cbw_ref, cbb_ref, wr_ref, wi_ref, br_ref, bi_ref, lam_ref,
                  wout_ref, wq_ref, wo_ref,
                  ya_ref, yb_ref, wout_bf_ref, wq_bf_ref, wo_bf_ref,
                  win_bf, cv_tail, xb_tail, h_carry):
    p, i = pl.program_id(0), pl.program_id(1)
    t_rows = ya_ref.shape[0]
    bf16 = jnp.bfloat16

    def cast_riders():
        for src, dst in ((wout_ref, wout_bf_ref), (wq_ref, wq_bf_ref), (wo_ref, wo_bf_ref)):
            dst[...] = src[...].astype(bf16)

    def cast_w_in_piece():
        piece = wv_ref.shape[0]
        w_next = win_bf.at[p % 2]
        for g, w_ref in enumerate((wv_ref, wb_ref, wc_ref, wga_ref, wxb_ref, wgb_ref)):
            w_next[g, pl.ds(i * piece, piece), :] = w_ref[...].astype(bf16)

    @pl.when(p == 0)
    def _():
        cast_riders()
        cast_w_in_piece()

    @pl.when(p > 0)
    def _():
        @pl.when(i == 0)
        def _():
            cv_tail[...] = jnp.zeros_like(cv_tail)
            xb_tail[...] = jnp.zeros_like(xb_tail)
            h_carry[...] = jnp.zeros_like(h_carry)

        cast_riders()
        neg_lam = -lam_ref[...]
        softplus_neg_lam = jnp.maximum(neg_lam, 0.0) + jnp.log1p(jnp.exp(-jnp.abs(neg_lam)))
        lam_scale = -LRU_C * softplus_neg_lam
        w_cur = win_bf.at[(p - 1) % 2]
        w_in = [w_cur.at[g] for g in range(6)]
        w_ri = [jnp.concatenate([wr_ref[hh].astype(bf16), wi_ref[hh].astype(bf16)], axis=1)
                for hh in range(wr_ref.shape[0])]
        chunk = MIX_ROW_CHUNK
        assert t_rows == 2 * chunk
        n_split = MIX_CH_TILE // MXU_COLS
        n_sub = chunk // MIX_SUB_ROWS
        a_rows = chunk // (len(_RECURRENCE_GROUPS) * n_split)
        never = p < 0
        cv_t, xb_t, h_prev = cv_tail[...], xb_tail[...], h_carry[...]

        def normed(c):
            return _rms_norm(x_ref[pl.ds(c * chunk, chunk), :], g_ref[...]).astype(bf16)

        def half(hn, k, hf):
            return _dot(hn, w_in[k][:, pl.ds(hf * MXU_COLS, MXU_COLS)])

        def tied(carry, part):
            rows = part[chunk - SUBLANES:]
            return jnp.where(never, jnp.concatenate([rows] * n_split, axis=1), carry)

        def gated_conv(c, parts, hf, cv_half, pieces, under):
            lanes = slice(hf * MXU_COLS, (hf + 1) * MXU_COLS)
            for q in pieces:
                sl = slice(q * a_rows, (q + 1) * a_rows)
                part = under()
                v_a, b_a, c_a, g_a = (parts[k][hf][sl] for k in (_V, _B, _C, _GA))
                y_a, cv_half = _group_a_piece(v_a, b_a, c_a, g_a, caw_ref[:, lanes],
                                              cab_ref[:, lanes], cv_half)
                ya_ref[pl.ds(c * chunk + sl.start, a_rows), lanes] = y_a.astype(ya_ref.dtype)
                if part is not None:
                    cv_half = jnp.where(never, part[chunk - SUBLANES:], cv_half)
            return cv_half

        def recurrence(c, hn, xb, g_b, xb_tail_c, h_prev, cv_half, first, extra=None):
            todo = [(k, hf) for hf in range(n_split) for k in _CONV_GROUPS]
            parts = {k: [None] * n_split for k in _CONV_GROUPS}
            xc_rows, r_rows, i_rows, tail = [], [], [], xb_tail_c
            for r0 in range(0, chunk, MIX_GATE_ROWS):
                xb_r = xb[r0:r0 + MIX_GATE_ROWS]
                xc_rows.append(_conv4(xb_r, tail, cbw_ref[...], cbb_ref[...]))
                tail = xb_r[MIX_GATE_ROWS - SUBLANES:]
                r_r, i_r = _gates(xc_rows[-1], w_ri)
                r_rows.append(r_r)
                i_rows.append(i_r)
            xc, r_pre, i_pre = (jnp.concatenate(v, axis=0) for v in (xc_rows, r_rows, i_rows))
            n_first = len(_CONV_GROUPS)
            for s in range(n_sub):
                sl = slice(s * MIX_SUB_ROWS, (s + 1) * MIX_SUB_ROWS)
                k, hf = todo.pop(0) if todo else (None, None)
                if k is not None:
                    parts[k][hf] = half(hn, k, hf)
                y_b, h_prev = _group_b_piece(xc[sl], r_pre[sl], i_pre[sl], g_b[sl], br_ref[...],
                                             bi_ref[...], lam_scale, h_prev, first if s == 0 else None)
                yb_ref[pl.ds(c * chunk + sl.start, MIX_SUB_ROWS), :] = y_b.astype(yb_ref.dtype)
                if n_first <= s < n_first + chunk // a_rows:
                    cv_half = gated_conv(c, parts, 0, cv_half, [s - n_first], lambda: None)
                if k is not None:
                    h_prev = tied(h_prev, parts[k][hf])
                if extra is not None and s == 0:
                    h_prev = jnp.where(never, extra(), h_prev)
            assert not todo
            return parts, h_prev, cv_half

        assert n_split == 2
        cv_halves = [cv_t[:, :MXU_COLS], cv_t[:, MXU_COLS:]]
        hn0 = normed(0)
        xb0, gb0 = _dot(hn0, w_in[_XB][...]), _dot(hn0, w_in[_GB][...])
        hn1_box = []

        def norm_chunk1():
            hn1_box.append(normed(1))
            return hn1_box[0][chunk - SUBLANES:, :MIX_CH_TILE].astype(jnp.float32)

        parts0, h_prev, cv_halves[0] = recurrence(0, hn0, xb0, gb0, xb_t, h_prev, cv_halves[0], i == 0,
                                                  norm_chunk1)
        hn1 = hn1_box[0]

        todo1 = [(k, hf) for k in _RECURRENCE_GROUPS for hf in range(n_split)]
        parts1 = {k: [None] * n_split for k in _RECURRENCE_GROUPS}

        def next_recurrence_half():
            k, hf = todo1.pop(0)
            parts1[k][hf] = half(hn1, k, hf)
            return parts1[k][hf]

        all_pieces = range(chunk // a_rows)
        cv_halves[1] = gated_conv(0, parts0, 1, cv_halves[1], all_pieces, next_recurrence_half)
        assert not todo1
        xb1, gb1 = (jnp.concatenate(parts1[k], axis=1) for k in _RECURRENCE_GROUPS)
        parts1a, h_prev, cv_halves[0] = recurrence(1, hn1, xb1, gb1, xb0[chunk - SUBLANES:], h_prev,
                                                   cv_halves[0], None)
        cv_halves[1] = gated_conv(1, parts1a, 1, cv_halves[1], all_pieces, lambda: None)
        cv_tail[...] = jnp.concatenate(cv_halves, axis=1)
        xb_tail[...], h_carry[...] = xb1[chunk - SUBLANES:], h_prev
        cast_w_in_piece()


def _attn_kernel(x_ref, ya_ref, yb_ref, woa_ref, wob_ref, gx_ref, wq_ref, k_ref, v_ref,
                 wo_ref, gf_ref, out_ref):
    h = x_ref[...] + _dot(ya_ref[...], woa_ref[...]) + _dot(yb_ref[...], wob_ref[...])
    hn = _rms_norm(h, gx_ref[...]).astype(jnp.bfloat16)
    q = _dot(hn, wq_ref[...]).astype(jnp.bfloat16)
    scale = XATTN_HEAD_DIM ** -0.5
    o_parts = []
    for hd in range(XATTN_HEADS):
        sl = slice(hd * XATTN_HEAD_DIM, (hd + 1) * XATTN_HEAD_DIM)
        s = lax.dot_general(q[:, sl], k_ref[:, sl], (((1,), (1,)), ((), ())),
                            preferred_element_type=jnp.float32) * scale
        p = jnp.exp(s - jnp.max(s, axis=-1, keepdims=True))
        p = p / jnp.sum(p, axis=-1, keepdims=True)
        o_parts.append(_dot(p.astype(jnp.bfloat16), v_ref[:, sl]))
    o = jnp.concatenate(o_parts, axis=1).astype(jnp.bfloat16)
    h2 = h + _dot(o, wo_ref[...])
    out_ref[...] = _rms_norm(h2, gf_ref[...]).astype(out_ref.dtype)


def _resident(shape):
    return pl.BlockSpec(shape, lambda *_: (0,) * len(shape), pipeline_mode=pl.Buffered(1))


def _layer(h, mem, norm_mix_g, w_in, conv_a_w, conv_a_b, conv_b_w, conv_b_b, w_rgate, b_rgate,
           w_igate, b_igate, lru_lambda, w_out, norm_x_g, norm_mem_g, w_q, w_kv, w_o, norm_f_g):
    seq, d = h.shape
    n_mem = mem.shape[0]
    bf16 = jnp.bfloat16
    row = lambda v: v.reshape(1, -1)

    kv = pl.pallas_call(
        _kv_kernel,
        grid=(2 * d // KV_COL_TILE,),
        in_specs=[pl.BlockSpec((n_mem, d), lambda j: (0, 0)),
                  pl.BlockSpec((1, d), lambda j: (0, 0)),
                  pl.BlockSpec((d, KV_COL_TILE), lambda j: (0, j))],
        out_specs=pl.BlockSpec((n_mem, KV_COL_TILE), lambda j: (0, j)),
        out_shape=jax.ShapeDtypeStruct((n_mem, 2 * d), bf16),
        compiler_params=pltpu.CompilerParams(dimension_semantics=("arbitrary",),
                                             vmem_limit_bytes=VMEM_LIMIT_BYTES),
        name="kv_proj",
    )(mem, row(norm_mem_g), w_kv)

    tm, cb = MIX_SEQ_TILE, MIX_CH_TILE
    n_cb, n_t = d // cb, seq // tm
    n_steps = n_cb * n_t
    piece = d // n_t
    blk = lambda p: jnp.maximum(p - 1, 0)
    nxt = lambda p: jnp.minimum(p, n_cb - 1)
    step = lambda p, i: jnp.minimum(p * n_t + i, n_steps - 1)
    tile = lambda p, i: jnp.where(p > 0, i, 0)
    col = lambda g: pl.BlockSpec((piece, cb), lambda p, i, g=g: (i, g * n_cb + nxt(p)))
    chan = lambda r: pl.BlockSpec((r, cb), lambda p, i: (0, blk(p)))
    gate = pl.BlockSpec((cb // HEAD_DIM, HEAD_DIM, HEAD_DIM), lambda p, i: (blk(p), 0, 0))
    y_spec = pl.BlockSpec((tm, cb), lambda p, i: (tile(p, i), blk(p)))
    rows_of = lambda w: pl.BlockSpec((w.shape[0] // n_steps, d), lambda p, i: (step(p, i), 0))
    y_a, y_b, w_out_bf, w_q_bf, w_o_bf = pl.pallas_call(
        _mixer_kernel,
        grid=(n_cb + 1, n_t),
        in_specs=[pl.BlockSpec((tm, d), lambda p, i: (tile(p, i), 0)),
                  pl.BlockSpec((1, d), lambda p, i: (0, 0)),
                  col(0), col(1), col(2), col(3), col(4), col(5),
                  chan(conv_a_w.shape[0]), chan(1), chan(conv_b_w.shape[0]), chan(1),
                  gate, gate, chan(1), chan(1), chan(1),
                  rows_of(w_out), rows_of(w_q), rows_of(w_o)],
        out_specs=[y_spec, y_spec, rows_of(w_out), rows_of(w_q), rows_of(w_o)],
        out_shape=[jax.ShapeDtypeStruct((seq, d), bf16), jax.ShapeDtypeStruct((seq, d), bf16),
                   jax.ShapeDtypeStruct(w_out.shape, bf16), jax.ShapeDtypeStruct(w_q.shape, bf16),
                   jax.ShapeDtypeStruct(w_o.shape, bf16)],
        scratch_shapes=[pltpu.VMEM((2, 6, d, cb), bf16),
                        pltpu.VMEM((SUBLANES, cb), jnp.float32),
                        pltpu.VMEM((SUBLANES, cb), jnp.float32),
                        pltpu.VMEM((SUBLANES, cb), jnp.float32)],
        compiler_params=pltpu.CompilerParams(dimension_semantics=("arbitrary", "arbitrary"),
                                             vmem_limit_bytes=VMEM_LIMIT_BYTES),
        name="mixer",
    )(h, row(norm_mix_g), w_in, w_in, w_in, w_in, w_in, w_in,
      conv_a_w, row(conv_a_b), conv_b_w, row(conv_b_b), w_rgate, w_igate,
      row(b_rgate), row(b_igate), row(lru_lambda), w_out, w_q, w_o)

    ta = ATTN_SEQ_TILE
    out = pl.pallas_call(
        _attn_kernel,
        grid=(seq // ta,),
        in_specs=[pl.BlockSpec((ta, d), lambda i: (i, 0)),
                  pl.BlockSpec((ta, d), lambda i: (i, 0)),
                  pl.BlockSpec((ta, d), lambda i: (i, 0)),
                  pl.BlockSpec((d, d), lambda i: (0, 0), pipeline_mode=pl.Buffered(1)),
                  pl.BlockSpec((d, d), lambda i: (1, 0), pipeline_mode=pl.Buffered(1)),
                  _resident((1, d)),
                  _resident((d, d)),
                  pl.BlockSpec((n_mem, d), lambda i: (0, 0), pipeline_mode=pl.Buffered(1)),
                  pl.BlockSpec((n_mem, d), lambda i: (0, 1), pipeline_mode=pl.Buffered(1)),
                  _resident((d, d)),
                  _resident((1, d))],
        out_specs=pl.BlockSpec((ta, d), lambda i: (i, 0)),
        out_shape=jax.ShapeDtypeStruct((seq, d), h.dtype),
        compiler_params=pltpu.CompilerParams(dimension_semantics=("arbitrary",),
                                             vmem_limit_bytes=VMEM_LIMIT_BYTES),
        name="outproj_xattn",
    )(h, y_a, y_b, w_out_bf, w_out_bf, row(norm_x_g), w_q_bf, kv, kv, w_o_bf, row(norm_f_g))
    return out


def kernel(x, mem, norm_mix_g, w_in, conv_a_w, conv_a_b, conv_b_w, conv_b_b, w_rgate, b_rgate,
           w_igate, b_igate, lru_lambda, w_out, norm_x_g, norm_mem_g, w_q, w_kv, w_o, norm_f_g):
    bsz, seq, d = x.shape
    depth = w_in.shape[0]
    assert bsz == 1 and depth == 1 and d == D_MODEL
    out = _layer(x[0], mem[0], norm_mix_g[0], w_in[0], conv_a_w[0], conv_a_b[0], conv_b_w[0],
                 conv_b_b[0], w_rgate[0], b_rgate[0], w_igate[0], b_igate[0], lru_lambda[0],
                 w_out[0], norm_x_g[0], norm_mem_g[0], w_q[0], w_kv[0], w_o[0], norm_f_g)
    return out[None]
```

```python
import jax
import jax.numpy as jnp
from jax import lax
from jax.experimental import pallas as pl
from jax.experimental.pallas import tpu as pltpu

D_MODEL = 2048
HEAD_DIM = 128
LRU_C = 8.0
XATTN_HEADS = 4
XATTN_HEAD_DIM = D_MODEL // XATTN_HEADS
RMS_EPS = 1e-6

SUBLANES = 8
MXU_COLS = 256
VMEM_LIMIT_BYTES = 56 * 1024 * 1024

MIX_SEQ_TILE = 512
MIX_ROW_CHUNK = 256
MIX_SUB_ROWS = 32
_V, _B, _C, _GA, _XB, _GB = range(6)
_RECURRENCE_GROUPS = (_XB, _GB)
_CONV_GROUPS = (_C, _V, _B, _GA)
MIX_CH_TILE = 512
ATTN_SEQ_TILE = 256
KV_COL_TILE = 512


def _rms_norm(xf, g):
    ms = jnp.mean(xf * xf, axis=-1, keepdims=True)
    return xf * lax.rsqrt(ms + RMS_EPS) * g


def _dot(a, b):
    return jnp.dot(a, b, preferred_element_type=jnp.float32)


def _kv_kernel(mem_ref, g_ref, wkv_ref, kv_ref):
    mn = _rms_norm(mem_ref[...], g_ref[...]).astype(jnp.bfloat16)
    kv_ref[...] = _dot(mn, wkv_ref[...].astype(jnp.bfloat16)).astype(kv_ref.dtype)


def _group_rows(u):
    return u.reshape(u.shape[0] // SUBLANES, SUBLANES, u.shape[1])


def _shift_rows(u, tail, s):
    rows, ch = u.shape
    rolled = pltpu.roll(_group_rows(u), s, axis=1)
    before = jnp.concatenate([pltpu.roll(tail, s, axis=0)[None], rolled[:-1]], axis=0)
    row = lax.broadcasted_iota(jnp.int32, rolled.shape, 1)
    return jnp.where(row < s, before, rolled).reshape(rows, ch)


def _linear_scan(a, b, h_prev):
    t_rows, ch = a.shape
    a, b = _group_rows(a), _group_rows(b)
    row = lax.broadcasted_iota(jnp.int32, a.shape, 1)
    for s in (1, 2, 4):
        keep = row >= s
        b = a * jnp.where(keep, pltpu.roll(b, s, axis=1), 0.0) + b
        a = a * jnp.where(keep, pltpu.roll(a, s, axis=1), 1.0)
    outs = []
    for g in range(t_rows // SUBLANES):
        h_g = a[g] * h_prev + b[g]
        outs.append(h_g)
        h_prev = jnp.broadcast_to(h_g[SUBLANES - 1:SUBLANES, :], (SUBLANES, ch))
    return jnp.concatenate(outs, axis=0), h_prev


def _silu(g):
    return g * jax.nn.sigmoid(g)


def _conv_gates(xb, xb_tail, cbw, cbb, w_ri):
    xc = (cbb + cbw[3:4] * xb
          + cbw[2:3] * _shift_rows(xb, xb_tail, 1)
          + cbw[1:2] * _shift_rows(xb, xb_tail, 2)
          + cbw[0:1] * _shift_rows(xb, xb_tail, 3))
    xc_bf = xc.astype(jnp.bfloat16)
    r_parts, i_parts = [], []
    for hh in range(xb.shape[1] // HEAD_DIM):
        ri = _dot(xc_bf[:, hh * HEAD_DIM:(hh + 1) * HEAD_DIM], w_ri[hh])
        r_parts.append(ri[:, :HEAD_DIM])
        i_parts.append(ri[:, HEAD_DIM:])
    return xc, jnp.concatenate(r_parts, axis=1), jnp.concatenate(i_parts, axis=1)


def _group_a_piece(v_a, b_a, c_a, g_a, caw, cab, cv_tail):
    cv = c_a * v_a
    conv = (cab + caw[2:3] * cv
            + caw[1:2] * _shift_rows(cv, cv_tail, 1)
            + caw[0:1] * _shift_rows(cv, cv_tail, 2))
    return b_a * conv * _silu(g_a), cv[cv.shape[0] - SUBLANES:]


def _group_b_piece(xc, r_pre, i_pre, g_b, b_r, b_i, lam_scale, h_prev, is_sequence_start):
    r = jax.nn.sigmoid(r_pre + b_r)
    i_gate = jax.nn.sigmoid(i_pre + b_i)
    log_a = lam_scale * r
    a = jnp.exp(log_a)
    th = jnp.tanh(log_a)
    mult = jnp.sqrt(-2.0 * th / (1.0 - th))
    if is_sequence_start is not None:
        first_row = (lax.broadcasted_iota(jnp.int32, mult.shape, 0) == 0) & is_sequence_start
        mult = jnp.where(first_row, 1.0, mult)
    h, h_prev = _linear_scan(a, mult * i_gate * xc, h_prev)
    return h * _silu(g_b), h_prev


def _mixer_kernel(x_ref, g_ref, wv_ref, wb_ref, wc_ref, wga_ref, wxb_ref, wgb_ref,
                  caw_ref, cab_ref, cbw_ref, cbb_ref, wr_ref, wi_ref, br_ref, bi_ref, lam_ref,
                  wout_ref, wq_ref, wo_ref,
                  ya_ref, yb_ref, wout_bf_ref, wq_bf_ref, wo_bf_ref,
                  win_bf, cv_tail, xb_tail, h_carry):
    p, i = pl.program_id(0), pl.program_id(1)
    t_rows = ya_ref.shape[0]
    bf16 = jnp.bfloat16

    def cast_riders():
        for src, dst in ((wout_ref, wout_bf_ref), (wq_ref, wq_bf_ref), (wo_ref, wo_bf_ref)):
            dst[...] = src[...].astype(bf16)

    def cast_w_in_piece():
        piece = wv_ref.shape[0]
        w_next = win_bf.at[p % 2]
        for g, w_ref in enumerate((wv_ref, wb_ref, wc_ref, wga_ref, wxb_ref, wgb_ref)):
            w_next[g, pl.ds(i * piece, piece), :] = w_ref[...].astype(bf16)

    @pl.when(p == 0)
    def _():
        cast_riders()
        cast_w_in_piece()

    @pl.when(p > 0)
    def _():
        @pl.when(i == 0)
        def _():
            cv_tail[...] = jnp.zeros_like(cv_tail)
            xb_tail[...] = jnp.zeros_like(xb_tail)
            h_carry[...] = jnp.zeros_like(h_carry)

        cast_riders()
        neg_lam = -lam_ref[...]
        softplus_neg_lam = jnp.maximum(neg_lam, 0.0) + jnp.log1p(jnp.exp(-jnp.abs(neg_lam)))
        lam_scale = -LRU_C * softplus_neg_lam
        w_cur = win_bf.at[(p - 1) % 2]
        w_in = [w_cur.at[g] for g in range(6)]
        w_ri = [jnp.concatenate([wr_ref[hh].astype(bf16), wi_ref[hh].astype(bf16)], axis=1)
                for hh in range(wr_ref.shape[0])]
        chunk = MIX_ROW_CHUNK
        assert t_rows == 2 * chunk
        n_split = MIX_CH_TILE // MXU_COLS
        n_sub = chunk // MIX_SUB_ROWS
        a_rows = chunk // (len(_RECURRENCE_GROUPS) * n_split)
        never = p < 0
        cv_t, xb_t, h_prev = cv_tail[...], xb_tail[...], h_carry[...]

        def normed(c):
            return _rms_norm(x_ref[pl.ds(c * chunk, chunk), :], g_ref[...]).astype(bf16)

        def half(hn, k, hf):
            return _dot(hn, w_in[k][:, pl.ds(hf * MXU_COLS, MXU_COLS)])

        def tied(carry, part):
            rows = part[chunk - SUBLANES:]
            return jnp.where(never, jnp.concatenate([rows] * n_split, axis=1), carry)

        def gated_conv(c, parts, hf, cv_half, pieces, under):
            lanes = slice(hf * MXU_COLS, (hf + 1) * MXU_COLS)
            for q in pieces:
                sl = slice(q * a_rows, (q + 1) * a_rows)
                part = under()
                v_a, b_a, c_a, g_a = (parts[k][hf][sl] for k in (_V, _B, _C, _GA))
                y_a, cv_half = _group_a_piece(v_a, b_a, c_a, g_a, caw_ref[:, lanes],
                                              cab_ref[:, lanes], cv_half)
                ya_ref[pl.ds(c * chunk + sl.start, a_rows), lanes] = y_a.astype(ya_ref.dtype)
                if part is not None:
                    cv_half = jnp.where(never, part[chunk - SUBLANES:], cv_half)
            return cv_half

        def recurrence(c, hn, xb, g_b, xb_tail_c, h_prev, cv_half, first):
            xc, r_pre, i_pre = _conv_gates(xb, xb_tail_c, cbw_ref[...], cbb_ref[...], w_ri)
            todo = [(k, hf) for hf in range(n_split) for k in _CONV_GROUPS]
            parts = {k: [None] * n_split for k in _CONV_GROUPS}
            n_first = len(_CONV_GROUPS)
            for s in range(n_sub):
                sl = slice(s * MIX_SUB_ROWS, (s + 1) * MIX_SUB_ROWS)
                k, hf = todo.pop(0)
                parts[k][hf] = half(hn, k, hf)
                y_b, h_prev = _group_b_piece(xc[sl], r_pre[sl], i_pre[sl], g_b[sl], br_ref[...],
                                             bi_ref[...], lam_scale, h_prev, first if s == 0 else None)
                yb_ref[pl.ds(c * chunk + sl.start, MIX_SUB_ROWS), :] = y_b.astype(yb_ref.dtype)
                if s >= n_first:
                    share = (chunk // a_rows) // (n_sub - n_first)
                    done = (s - n_first) * share
                    cv_half = gated_conv(c, parts, 0, cv_half, range(done, done + share), lambda: None)
                h_prev = tied(h_prev, parts[k][hf])
            assert not todo
            return parts, h_prev, cv_half

        assert n_split == 2
        cv_halves = [cv_t[:, :MXU_COLS], cv_t[:, MXU_COLS:]]
        hn0, hn1 = normed(0), normed(1)
        xb0, gb0 = _dot(hn0, w_in[_XB][...]), _dot(hn0, w_in[_GB][...])
        parts0, h_prev, cv_halves[0] = recurrence(0, hn0, xb0, gb0, xb_t, h_prev, cv_halves[0], i == 0)

        todo1 = [(k, hf) for k in _RECURRENCE_GROUPS for hf in range(n_split)]
        parts1 = {k: [None] * n_split for k in _RECURRENCE_GROUPS}

        def next_recurrence_half():
            k, hf = todo1.pop(0)
            parts1[k][hf] = half(hn1, k, hf)
            return parts1[k][hf]

        all_pieces = range(chunk // a_rows)
        cv_halves[1] = gated_conv(0, parts0, 1, cv_halves[1], all_pieces, next_recurrence_half)
        assert not todo1
        xb1, gb1 = (jnp.concatenate(parts1[k], axis=1) for k in _RECURRENCE_GROUPS)
        parts1a, h_prev, cv_halves[0] = recurrence(1, hn1, xb1, gb1, xb0[chunk - SUBLANES:], h_prev,
                                                   cv_halves[0], None)
        cv_halves[1] = gated_conv(1, parts1a, 1, cv_halves[1], all_pieces, lambda: None)
        cv_tail[...] = jnp.concatenate(cv_halves, axis=1)
        xb_tail[...], h_carry[...] = xb1[chunk - SUBLANES:], h_prev
        cast_w_in_piece()


def _attn_kernel(x_ref, ya_ref, yb_ref, woa_ref, wob_ref, gx_ref, wq_ref, k_ref, v_ref,
                 wo_ref, gf_ref, out_ref):
    h = x_ref[...] + _dot(ya_ref[...], woa_ref[...]) + _dot(yb_ref[...], wob_ref[...])
    hn = _rms_norm(h, gx_ref[...]).astype(jnp.bfloat16)
    q = _dot(hn, wq_ref[...]).astype(jnp.bfloat16)
    scale = XATTN_HEAD_DIM ** -0.5
    o_parts = []
    for hd in range(XATTN_HEADS):
        sl = slice(hd * XATTN_HEAD_DIM, (hd + 1) * XATTN_HEAD_DIM)
        s = lax.dot_general(q[:, sl], k_ref[:, sl], (((1,), (1,)), ((), ())),
                            preferred_element_type=jnp.float32) * scale
        p = jnp.exp(s - jnp.max(s, axis=-1, keepdims=True))
        p = p / jnp.sum(p, axis=-1, keepdims=True)
        o_parts.append(_dot(p.astype(jnp.bfloat16), v_ref[:, sl]))
    o = jnp.concatenate(o_parts, axis=1).astype(jnp.bfloat16)
    h2 = h + _dot(o, wo_ref[...])
    out_ref[...] = _rms_norm(h2, gf_ref[...]).astype(out_ref.dtype)


def _resident(shape):
    return pl.BlockSpec(shape, lambda *_: (0,) * len(shape), pipeline_mode=pl.Buffered(1))


def _layer(h, mem, norm_mix_g, w_in, conv_a_w, conv_a_b, conv_b_w, conv_b_b, w_rgate, b_rgate,
           w_igate, b_igate, lru_lambda, w_out, norm_x_g, norm_mem_g, w_q, w_kv, w_o, norm_f_g):
    seq, d = h.shape
    n_mem = mem.shape[0]
    bf16 = jnp.bfloat16
    row = lambda v: v.reshape(1, -1)

    kv = pl.pallas_call(
        _kv_kernel,
        grid=(2 * d // KV_COL_TILE,),
        in_specs=[pl.BlockSpec((n_mem, d), lambda j: (0, 0)),
                  pl.BlockSpec((1, d), lambda j: (0, 0)),
                  pl.BlockSpec((d, KV_COL_TILE), lambda j: (0, j))],
        out_specs=pl.BlockSpec((n_mem, KV_COL_TILE), lambda j: (0, j)),
        out_shape=jax.ShapeDtypeStruct((n_mem, 2 * d), bf16),
        compiler_params=pltpu.CompilerParams(dimension_semantics=("arbitrary",),
                                             vmem_limit_bytes=VMEM_LIMIT_BYTES),
        name="kv_proj",
    )(mem, row(norm_mem_g), w_kv)

    tm, cb = MIX_SEQ_TILE, MIX_CH_TILE
    n_cb, n_t = d // cb, seq // tm
    n_steps = n_cb * n_t
    piece = d // n_t
    blk = lambda p: jnp.maximum(p - 1, 0)
    nxt = lambda p: jnp.minimum(p, n_cb - 1)
    step = lambda p, i: jnp.minimum(p * n_t + i, n_steps - 1)
    tile = lambda p, i: jnp.where(p > 0, i, 0)
    col = lambda g: pl.BlockSpec((piece, cb), lambda p, i, g=g: (i, g * n_cb + nxt(p)))
    chan = lambda r: pl.BlockSpec((r, cb), lambda p, i: (0, blk(p)))
    gate = pl.BlockSpec((cb // HEAD_DIM, HEAD_DIM, HEAD_DIM), lambda p, i: (blk(p), 0, 0))
    y_spec = pl.BlockSpec((tm, cb), lambda p, i: (tile(p, i), blk(p)))
    rows_of = lambda w: pl.BlockSpec((w.shape[0] // n_steps, d), lambda p, i: (step(p, i), 0))
    y_a, y_b, w_out_bf, w_q_bf, w_o_bf = pl.pallas_call(
        _mixer_kernel,
        grid=(n_cb + 1, n_t),
        in_specs=[pl.BlockSpec((tm, d), lambda p, i: (tile(p, i), 0)),
                  pl.BlockSpec((1, d), lambda p, i: (0, 0)),
                  col(0), col(1), col(2), col(3), col(4), col(5),
                  chan(conv_a_w.shape[0]), chan(1), chan(conv_b_w.shape[0]), chan(1),
                  gate, gate, chan(1), chan(1), chan(1),
                  rows_of(w_out), rows_of(w_q), rows_of(w_o)],
        out_specs=[y_spec, y_spec, rows_of(w_out), rows_of(w_q), rows_of(w_o)],
        out_shape=[jax.ShapeDtypeStruct((seq, d), bf16), jax.ShapeDtypeStruct((seq, d), bf16),
                   jax.ShapeDtypeStruct(w_out.shape, bf16), jax.ShapeDtypeStruct(w_q.shape, bf16),
                   jax.ShapeDtypeStruct(w_o.shape, bf16)],
        scratch_shapes=[pltpu.VMEM((2, 6, d, cb), bf16),
                        pltpu.VMEM((SUBLANES, cb), jnp.float32),
                        pltpu.VMEM((SUBLANES, cb), jnp.float32),
                        pltpu.VMEM((SUBLANES, cb), jnp.float32)],
        compiler_params=pltpu.CompilerParams(dimension_semantics=("arbitrary", "arbitrary"),
                                             vmem_limit_bytes=VMEM_LIMIT_BYTES),
        name="mixer",
    )(h, row(norm_mix_g), w_in, w_in, w_in, w_in, w_in, w_in,
      conv_a_w, row(conv_a_b), conv_b_w, row(conv_b_b), w_rgate, w_igate,
      row(b_rgate), row(b_igate), row(lru_lambda), w_out, w_q, w_o)

    ta = ATTN_SEQ_TILE
    out = pl.pallas_call(
        _attn_kernel,
        grid=(seq // ta,),
        in_specs=[pl.BlockSpec((ta, d), lambda i: (i, 0)),
                  pl.BlockSpec((ta, d), lambda i: (i, 0)),
                  pl.BlockSpec((ta, d), lambda i: (i, 0)),
                  pl.BlockSpec((d, d), lambda i: (0, 0), pipeline_mode=pl.Buffered(1)),
                  pl.BlockSpec((d, d), lambda i: (1, 0), pipeline_mode=pl.Buffered(1)),
                  _resident((1, d)),
                  _resident((d, d)),
                  pl.BlockSpec((n_mem, d), lambda i: (0, 0), pipeline_mode=pl.Buffered(1)),
                  pl.BlockSpec((n_mem, d), lambda i: (0, 1), pipeline_mode=pl.Buffered(1)),
                  _resident((d, d)),
                  _resident((1, d))],
        out_specs=pl.BlockSpec((ta, d), lambda i: (i, 0)),
        out_shape=jax.ShapeDtypeStruct((seq, d), h.dtype),
        compiler_params=pltpu.CompilerParams(dimension_semantics=("arbitrary",),
                                             vmem_limit_bytes=VMEM_LIMIT_BYTES),
        name="outproj_xattn",
    )(h, y_a, y_b, w_out_bf, w_out_bf, row(norm_x_g), w_q_bf, kv, kv, w_o_bf, row(norm_f_g))
    return out


def kernel(x, mem, norm_mix_g, w_in, conv_a_w, conv_a_b, conv_b_w, conv_b_b, w_rgate, b_rgate,
           w_igate, b_igate, lru_lambda, w_out, norm_x_g, norm_mem_g, w_q, w_kv, w_o, norm_f_g):
    bsz, seq, d = x.shape
    depth = w_in.shape[0]
    assert bsz == 1 and depth == 1 and d == D_MODEL
    out = _layer(x[0], mem[0], norm_mix_g[0], w_in[0], conv_a_w[0], conv_a_b[0], conv_b_w[0],
                 conv_b_b[0], w_rgate[0], b_rgate[0], w_igate[0], b_igate[0], lru_lambda[0],
                 w_out[0], norm_x_g[0], norm_mem_g[0], w_q[0], w_kv[0], w_o[0], norm_f_g)
    return out[None]
```

```python
import jax
import jax.numpy as jnp
from jax import lax
from jax.experimental import pallas as pl
from jax.experimental.pallas import tpu as pltpu

D_MODEL = 2048
HEAD_DIM = 128
LRU_C = 8.0
XATTN_HEADS = 4
XATTN_HEAD_DIM = D_MODEL // XATTN_HEADS
RMS_EPS = 1e-6

SUBLANES = 8
MXU_COLS = 256
VMEM_LIMIT_BYTES = 56 * 1024 * 1024

MIX_SEQ_TILE = 512
MIX_ROW_CHUNK = 256
MIX_SUB_ROWS = 32
_V, _B, _C, _GA, _XB, _GB = range(6)
_RECURRENCE_GROUPS = (_XB, _GB)
_CONV_GROUPS = (_C, _V, _B, _GA)
MIX_CH_TILE = 512
ATTN_SEQ_TILE = 256
KV_COL_TILE = 512


def _rms_norm(xf, g):
    ms = jnp.mean(xf * xf, axis=-1, keepdims=True)
    return xf * lax.rsqrt(ms + RMS_EPS) * g


def _dot(a, b):
    return jnp.dot(a, b, preferred_element_type=jnp.float32)


def _kv_kernel(mem_ref, g_ref, wkv_ref, kv_ref):
    mn = _rms_norm(mem_ref[...], g_ref[...]).astype(jnp.bfloat16)
    kv_ref[...] = _dot(mn, wkv_ref[...].astype(jnp.bfloat16)).astype(kv_ref.dtype)


def _group_rows(u):
    return u.reshape(u.shape[0] // SUBLANES, SUBLANES, u.shape[1])


def _shift_rows(u, tail, s):
    rows, ch = u.shape
    rolled = pltpu.roll(_group_rows(u), s, axis=1)
    before = jnp.concatenate([pltpu.roll(tail, s, axis=0)[None], rolled[:-1]], axis=0)
    row = lax.broadcasted_iota(jnp.int32, rolled.shape, 1)
    return jnp.where(row < s, before, rolled).reshape(rows, ch)


def _linear_scan(a, b, h_prev):
    t_rows, ch = a.shape
    a, b = _group_rows(a), _group_rows(b)
    row = lax.broadcasted_iota(jnp.int32, a.shape, 1)
    for s in (1, 2, 4):
        keep = row >= s
        b = a * jnp.where(keep, pltpu.roll(b, s, axis=1), 0.0) + b
        a = a * jnp.where(keep, pltpu.roll(a, s, axis=1), 1.0)
    outs = []
    for g in range(t_rows // SUBLANES):
        h_g = a[g] * h_prev + b[g]
        outs.append(h_g)
        h_prev = jnp.broadcast_to(h_g[SUBLANES - 1:SUBLANES, :], (SUBLANES, ch))
    return jnp.concatenate(outs, axis=0), h_prev


def _silu(g):
    return g * jax.nn.sigmoid(g)


def _conv_gates(xb, xb_tail, cbw, cbb, w_ri):
    xc = (cbb + cbw[3:4] * xb
          + cbw[2:3] * _shift_rows(xb, xb_tail, 1)
          + cbw[1:2] * _shift_rows(xb, xb_tail, 2)
          + cbw[0:1] * _shift_rows(xb, xb_tail, 3))
    xc_bf = xc.astype(jnp.bfloat16)
    r_parts, i_parts = [], []
    for hh in range(xb.shape[1] // HEAD_DIM):
        ri = _dot(xc_bf[:, hh * HEAD_DIM:(hh + 1) * HEAD_DIM], w_ri[hh])
        r_parts.append(ri[:, :HEAD_DIM])
        i_parts.append(ri[:, HEAD_DIM:])
    return xc, jnp.concatenate(r_parts, axis=1), jnp.concatenate(i_parts, axis=1)


def _conv3(v_a, c_a, caw, cab, cv_tail):
    cv = c_a * v_a
    conv = (cab + caw[2:3] * cv
            + caw[1:2] * _shift_rows(cv, cv_tail, 1)
            + caw[0:1] * _shift_rows(cv, cv_tail, 2))
    return conv, cv[cv.shape[0] - SUBLANES:]


def _group_a_piece(v_a, b_a, c_a, g_a, caw, cab, cv_tail):
    conv, cv_tail = _conv3(v_a, c_a, caw, cab, cv_tail)
    return b_a * conv * _silu(g_a), cv_tail


def _group_b_piece(xc, r_pre, i_pre, g_b, b_r, b_i, lam_scale, h_prev, is_sequence_start):
    r = jax.nn.sigmoid(r_pre + b_r)
    i_gate = jax.nn.sigmoid(i_pre + b_i)
    log_a = lam_scale * r
    a = jnp.exp(log_a)
    th = jnp.tanh(log_a)
    mult = jnp.sqrt(-2.0 * th / (1.0 - th))
    if is_sequence_start is not None:
        first_row = (lax.broadcasted_iota(jnp.int32, mult.shape, 0) == 0) & is_sequence_start
        mult = jnp.where(first_row, 1.0, mult)
    h, h_prev = _linear_scan(a, mult * i_gate * xc, h_prev)
    return h * _silu(g_b), h_prev


def _mixer_kernel(x_ref, g_ref, wv_ref, wb_ref, wc_ref, wga_ref, wxb_ref, wgb_ref,
                  caw_ref, cab_ref, cbw_ref, cbb_ref, wr_ref, wi_ref, br_ref, bi_ref, lam_ref,
                  wout_ref, wq_ref, wo_ref,
                  ya_ref, yb_ref, wout_bf_ref, wq_bf_ref, wo_bf_ref,
                  win_bf, cv_tail, xb_tail, h_carry):
    p, i = pl.program_id(0), pl.program_id(1)
    t_rows = ya_ref.shape[0]
    bf16 = jnp.bfloat16

    def cast_riders():
        for src, dst in ((wout_ref, wout_bf_ref), (wq_ref, wq_bf_ref), (wo_ref, wo_bf_ref)):
            dst[...] = src[...].astype(bf16)

    def cast_w_in_piece():
        piece = wv_ref.shape[0]
        w_next = win_bf.at[p % 2]
        for g, w_ref in enumerate((wv_ref, wb_ref, wc_ref, wga_ref, wxb_ref, wgb_ref)):
            w_next[g, pl.ds(i * piece, piece), :] = w_ref[...].astype(bf16)

    @pl.when(p == 0)
    def _():
        cast_riders()
        cast_w_in_piece()

    @pl.when(p > 0)
    def _():
        @pl.when(i == 0)
        def _():
            cv_tail[...] = jnp.zeros_like(cv_tail)
            xb_tail[...] = jnp.zeros_like(xb_tail)
            h_carry[...] = jnp.zeros_like(h_carry)

        cast_riders()
        neg_lam = -lam_ref[...]
        softplus_neg_lam = jnp.maximum(neg_lam, 0.0) + jnp.log1p(jnp.exp(-jnp.abs(neg_lam)))
        lam_scale = -LRU_C * softplus_neg_lam
        w_cur = win_bf.at[(p - 1) % 2]
        w_in = [w_cur.at[g] for g in range(6)]
        w_ri = [jnp.concatenate([wr_ref[hh].astype(bf16), wi_ref[hh].astype(bf16)], axis=1)
                for hh in range(wr_ref.shape[0])]
        chunk = MIX_ROW_CHUNK
        assert t_rows == 2 * chunk
        n_split = MIX_CH_TILE // MXU_COLS
        n_sub = chunk // MIX_SUB_ROWS
        a_rows = chunk // (len(_RECURRENCE_GROUPS) * n_split)
        never = p < 0
        cv_t, xb_t, h_prev = cv_tail[...], xb_tail[...], h_carry[...]

        def normed(c):
            return _rms_norm(x_ref[pl.ds(c * chunk, chunk), :], g_ref[...]).astype(bf16)

        def half(hn, k, hf):
            return _dot(hn, w_in[k][:, pl.ds(hf * MXU_COLS, MXU_COLS)])

        def tied(carry, part):
            rows = part[chunk - SUBLANES:]
            return jnp.where(never, jnp.concatenate([rows] * n_split, axis=1), carry)

        def gated_conv(c, parts, hf, cv_half, pieces, under):
            lanes = slice(hf * MXU_COLS, (hf + 1) * MXU_COLS)
            for q in pieces:
                sl = slice(q * a_rows, (q + 1) * a_rows)
                part = under()
                v_a, b_a, c_a, g_a = (parts[k][hf][sl] for k in (_V, _B, _C, _GA))
                y_a, cv_half = _group_a_piece(v_a, b_a, c_a, g_a, caw_ref[:, lanes],
                                              cab_ref[:, lanes], cv_half)
                ya_ref[pl.ds(c * chunk + sl.start, a_rows), lanes] = y_a.astype(ya_ref.dtype)
                if part is not None:
                    cv_half = jnp.where(never, part[chunk - SUBLANES:], cv_half)
            return cv_half

        def recurrence(c, hn, xb, g_b, xb_tail_c, h_prev, cv_half, first, cv_other=None):
            xc, r_pre, i_pre = _conv_gates(xb, xb_tail_c, cbw_ref[...], cbb_ref[...], w_ri)
            todo = [(k, hf) for hf in range(n_split) for k in _CONV_GROUPS]
            parts = {k: [None] * n_split for k in _CONV_GROUPS}
            n_first = len(_CONV_GROUPS)
            n_late = n_sub - n_first - 2
            per_late = (chunk // a_rows) // n_late
            late = []
            for s in range(n_sub):
                sl = slice(s * MIX_SUB_ROWS, (s + 1) * MIX_SUB_ROWS)
                k, hf = todo.pop(0)
                parts[k][hf] = half(hn, k, hf)
                y_b, h_prev = _group_b_piece(xc[sl], r_pre[sl], i_pre[sl], g_b[sl], br_ref[...],
                                             bi_ref[...], lam_scale, h_prev, first if s == 0 else None)
                yb_ref[pl.ds(c * chunk + sl.start, MIX_SUB_ROWS), :] = y_b.astype(yb_ref.dtype)
                if s >= n_first:
                    share = (chunk // a_rows) // (n_sub - n_first)
                    done = (s - n_first) * share
                    cv_half = gated_conv(c, parts, 0, cv_half, range(done, done + share), lambda: None)
                h_prev = tied(h_prev, parts[k][hf])
                if cv_other is not None and s >= n_sub - n_late:
                    lanes = slice(MXU_COLS, 2 * MXU_COLS)
                    for q in range((s - (n_sub - n_late)) * per_late, (s - (n_sub - n_late) + 1) * per_late):
                        rows = slice(q * a_rows, (q + 1) * a_rows)
                        conv, cv_other = _conv3(parts[_V][1][rows], parts[_C][1][rows],
                                                caw_ref[:, lanes], cab_ref[:, lanes], cv_other)
                        late.append(conv)
            assert not todo
            return parts, h_prev, cv_half, late, cv_other

        assert n_split == 2
        cv_halves = [cv_t[:, :MXU_COLS], cv_t[:, MXU_COLS:]]
        hn0, hn1 = normed(0), normed(1)
        xb0, gb0 = _dot(hn0, w_in[_XB][...]), _dot(hn0, w_in[_GB][...])
        parts0, h_prev, cv_halves[0], _, _ = recurrence(0, hn0, xb0, gb0, xb_t, h_prev, cv_halves[0],
                                                        i == 0)

        todo1 = [(k, hf) for k in _RECURRENCE_GROUPS for hf in range(n_split)]
        parts1 = {k: [None] * n_split for k in _RECURRENCE_GROUPS}

        def next_recurrence_half():
            k, hf = todo1.pop(0)
            parts1[k][hf] = half(hn1, k, hf)
            return parts1[k][hf]

        all_pieces = range(chunk // a_rows)
        cv_halves[1] = gated_conv(0, parts0, 1, cv_halves[1], all_pieces, next_recurrence_half)
        assert not todo1
        xb1, gb1 = (jnp.concatenate(parts1[k], axis=1) for k in _RECURRENCE_GROUPS)
        parts1a, h_prev, cv_halves[0], conv1, cv_halves[1] = recurrence(
            1, hn1, xb1, gb1, xb0[chunk - SUBLANES:], h_prev, cv_halves[0], None, cv_halves[1])
        for q, conv in enumerate(conv1):
            rows = slice(q * a_rows, (q + 1) * a_rows)
            y_a = parts1a[_B][1][rows] * conv * _silu(parts1a[_GA][1][rows])
            ya_ref[pl.ds(chunk + q * a_rows, a_rows), pl.ds(MXU_COLS, MXU_COLS)] = y_a.astype(ya_ref.dtype)
        cv_tail[...] = jnp.concatenate(cv_halves, axis=1)
        xb_tail[...], h_carry[...] = xb1[chunk - SUBLANES:], h_prev
        cast_w_in_piece()


def _attn_kernel(x_ref, ya_ref, yb_ref, woa_ref, wob_ref, gx_ref, wq_ref, k_ref, v_ref,
                 wo_ref, gf_ref, out_ref):
    h = x_ref[...] + _dot(ya_ref[...], woa_ref[...]) + _dot(yb_ref[...], wob_ref[...])
    hn = _rms_norm(h, gx_ref[...]).astype(jnp.bfloat16)
    q = _dot(hn, wq_ref[...]).astype(jnp.bfloat16)
    scale = XATTN_HEAD_DIM ** -0.5
    o_parts = []
    for hd in range(XATTN_HEADS):
        sl = slice(hd * XATTN_HEAD_DIM, (hd + 1) * XATTN_HEAD_DIM)
        s = lax.dot_general(q[:, sl], k_ref[:, sl], (((1,), (1,)), ((), ())),
                            preferred_element_type=jnp.float32) * scale
        p = jnp.exp(s - jnp.max(s, axis=-1, keepdims=True))
        p = p / jnp.sum(p, axis=-1, keepdims=True)
        o_parts.append(_dot(p.astype(jnp.bfloat16), v_ref[:, sl]))
    o = jnp.concatenate(o_parts, axis=1).astype(jnp.bfloat16)
    h2 = h + _dot(o, wo_ref[...])
    out_ref[...] = _rms_norm(h2, gf_ref[...]).astype(out_ref.dtype)


def _resident(shape):
    return pl.BlockSpec(shape, lambda *_: (0,) * len(shape), pipeline_mode=pl.Buffered(1))


def _layer(h, mem, norm_mix_g, w_in, conv_a_w, conv_a_b, conv_b_w, conv_b_b, w_rgate, b_rgate,
           w_igate, b_igate, lru_lambda, w_out, norm_x_g, norm_mem_g, w_q, w_kv, w_o, norm_f_g):
    seq, d = h.shape
    n_mem = mem.shape[0]
    bf16 = jnp.bfloat16
    row = lambda v: v.reshape(1, -1)

    kv = pl.pallas_call(
        _kv_kernel,
        grid=(2 * d // KV_COL_TILE,),
        in_specs=[pl.BlockSpec((n_mem, d), lambda j: (0, 0)),
                  pl.BlockSpec((1, d), lambda j: (0, 0)),
                  pl.BlockSpec((d, KV_COL_TILE), lambda j: (0, j))],
        out_specs=pl.BlockSpec((n_mem, KV_COL_TILE), lambda j: (0, j)),
        out_shape=jax.ShapeDtypeStruct((n_mem, 2 * d), bf16),
        compiler_params=pltpu.CompilerParams(dimension_semantics=("arbitrary",),
                                             vmem_limit_bytes=VMEM_LIMIT_BYTES),
        name="kv_proj",
    )(mem, row(norm_mem_g), w_kv)

    tm, cb = MIX_SEQ_TILE, MIX_CH_TILE
    n_cb, n_t = d // cb, seq // tm
    n_steps = n_cb * n_t
    piece = d // n_t
    blk = lambda p: jnp.maximum(p - 1, 0)
    nxt = lambda p: jnp.minimum(p, n_cb - 1)
    step = lambda p, i: jnp.minimum(p * n_t + i, n_steps - 1)
    tile = lambda p, i: jnp.where(p > 0, i, 0)
    col = lambda g: pl.BlockSpec((piece, cb), lambda p, i, g=g: (i, g * n_cb + nxt(p)))
    chan = lambda r: pl.BlockSpec((r, cb), lambda p, i: (0, blk(p)))
    gate = pl.BlockSpec((cb // HEAD_DIM, HEAD_DIM, HEAD_DIM), lambda p, i: (blk(p), 0, 0))
    y_spec = pl.BlockSpec((tm, cb), lambda p, i: (tile(p, i), blk(p)))
    rows_of = lambda w: pl.BlockSpec((w.shape[0] // n_steps, d), lambda p, i: (step(p, i), 0))
    y_a, y_b, w_out_bf, w_q_bf, w_o_bf = pl.pallas_call(
        _mixer_kernel,
        grid=(n_cb + 1, n_t),
        in_specs=[pl.BlockSpec((tm, d), lambda p, i: (tile(p, i), 0)),
                  pl.BlockSpec((1, d), lambda p, i: (0, 0)),
                  col(0), col(1), col(2), col(3), col(4), col(5),
                  chan(conv_a_w.shape[0]), chan(1), chan(conv_b_w.shape[0]), chan(1),
                  gate, gate, chan(1), chan(1), chan(1),
                  rows_of(w_out), rows_of(w_q), rows_of(w_o)],
        out_specs=[y_spec, y_spec, rows_of(w_out), rows_of(w_q), rows_of(w_o)],
        out_shape=[jax.ShapeDtypeStruct((seq, d), bf16), jax.ShapeDtypeStruct((seq, d), bf16),
                   jax.ShapeDtypeStruct(w_out.shape, bf16), jax.ShapeDtypeStruct(w_q.shape, bf16),
                   jax.ShapeDtypeStruct(w_o.shape, bf16)],
        scratch_shapes=[pltpu.VMEM((2, 6, d, cb), bf16),
                        pltpu.VMEM((SUBLANES, cb), jnp.float32),
                        pltpu.VMEM((SUBLANES, cb), jnp.float32),
                        pltpu.VMEM((SUBLANES, cb), jnp.float32)],
        compiler_params=pltpu.CompilerParams(dimension_semantics=("arbitrary", "arbitrary"),
                                             vmem_limit_bytes=VMEM_LIMIT_BYTES),
        name="mixer",
    )(h, row(norm_mix_g), w_in, w_in, w_in, w_in, w_in, w_in,
      conv_a_w, row(conv_a_b), conv_b_w, row(conv_b_b), w_rgate, w_igate,
      row(b_rgate), row(b_igate), row(lru_lambda), w_out, w_q, w_o)

    ta = ATTN_SEQ_TILE
    out = pl.pallas_call(
        _attn_kernel,
        grid=(seq // ta,),
        in_specs=[pl.BlockSpec((ta, d), lambda i: (i, 0)),
                  pl.BlockSpec((ta, d), lambda i: (i, 0)),
                  pl.BlockSpec((ta, d), lambda i: (i, 0)),
                  pl.BlockSpec((d, d), lambda i: (0, 0), pipeline_mode=pl.Buffered(1)),
                  pl.BlockSpec((d, d), lambda i: (1, 0), pipeline_mode=pl.Buffered(1)),
                  _resident((1, d)),
                  _resident((d, d)),
                  pl.BlockSpec((n_mem, d), lambda i: (0, 0), pipeline_mode=pl.Buffered(1)),
                  pl.BlockSpec((n_mem, d), lambda i: (0, 1), pipeline_mode=pl.Buffered(1)),
                  _resident((d, d)),
                  _resident((1, d))],
        out_specs=pl.BlockSpec((ta, d), lambda i: (i, 0)),
        out_shape=jax.ShapeDtypeStruct((seq, d), h.dtype),
        compiler_params=pltpu.CompilerParams(dimension_semantics=("arbitrary",),
                                             vmem_limit_bytes=VMEM_LIMIT_BYTES),
        name="outproj_xattn",
    )(h, y_a, y_b, w_out_bf, w_out_bf, row(norm_x_g), w_q_bf, kv, kv, w_o_bf, row(norm_f_g))
    return out


def kernel(x, mem, norm_mix_g, w_in, conv_a_w, conv_a_b, conv_b_w, conv_b_b, w_rgate, b_rgate,
           w_igate, b_igate, lru_lambda, w_out, norm_x_g, norm_mem_g, w_q, w_kv, w_o, norm_f_g):
    bsz, seq, d = x.shape
    depth = w_in.shape[0]
    assert bsz == 1 and depth == 1 and d == D_MODEL
    out = _layer(x[0], mem[0], norm_mix_g[0], w_in[0], conv_a_w[0], conv_a_b[0], conv_b_w[0],
                 conv_b_b[0], w_rgate[0], b_rgate[0], w_igate[0], b_igate[0], lru_lambda[0],
                 w_out[0], norm_x_g[0], norm_mem_g[0], w_q[0], w_kv[0], w_o[0], norm_f_g)
    return out[None]
```

```python
import jax
import jax.numpy as jnp
from jax import lax
from jax.experimental import pallas as pl
from jax.experimental.pallas import tpu as pltpu

D_MODEL = 2048
HEAD_DIM = 128
LRU_C = 8.0
XATTN_HEADS = 4
XATTN_HEAD_DIM = D_MODEL // XATTN_HEADS
RMS_EPS = 1e-6

SUBLANES = 8
MXU_COLS = 256
VMEM_LIMIT_BYTES = 56 * 1024 * 1024

MIX_SEQ_TILE = 512
MIX_ROW_CHUNK = 256
MIX_SUB_ROWS = 32
_V, _B, _C, _GA, _XB, _GB = range(6)
_RECURRENCE_GROUPS = (_XB, _GB)
_CONV_GROUPS = (_C, _V, _B, _GA)
MIX_CH_TILE = 512
ATTN_SEQ_TILE = 256
KV_COL_TILE = 512


def _rms_norm(xf, g):
    ms = jnp.mean(xf * xf, axis=-1, keepdims=True)
    return xf * lax.rsqrt(ms + RMS_EPS) * g


def _dot(a, b):
    return jnp.dot(a, b, preferred_element_type=jnp.float32)


def _kv_kernel(mem_ref, g_ref, wkv_ref, kv_ref):
    mn = _rms_norm(mem_ref[...], g_ref[...]).astype(jnp.bfloat16)
    kv_ref[...] = _dot(mn, wkv_ref[...].astype(jnp.bfloat16)).astype(kv_ref.dtype)


def _group_rows(u):
    return u.reshape(u.shape[0] // SUBLANES, SUBLANES, u.shape[1])


def _shift_rows(u, tail, s):
    rows, ch = u.shape
    rolled = pltpu.roll(_group_rows(u), s, axis=1)
    before = jnp.concatenate([pltpu.roll(tail, s, axis=0)[None], rolled[:-1]], axis=0)
    row = lax.broadcasted_iota(jnp.int32, rolled.shape, 1)
    return jnp.where(row < s, before, rolled).reshape(rows, ch)


def _linear_scan(a, b, h_prev):
    t_rows, ch = a.shape
    a, b = _group_rows(a), _group_rows(b)
    row = lax.broadcasted_iota(jnp.int32, a.shape, 1)
    for s in (1, 2, 4):
        keep = row >= s
        b = a * jnp.where(keep, pltpu.roll(b, s, axis=1), 0.0) + b
        a = a * jnp.where(keep, pltpu.roll(a, s, axis=1), 1.0)
    outs = []
    for g in range(t_rows // SUBLANES):
        h_g = a[g] * h_prev + b[g]
        outs.append(h_g)
        h_prev = jnp.broadcast_to(h_g[SUBLANES - 1:SUBLANES, :], (SUBLANES, ch))
    return jnp.concatenate(outs, axis=0), h_prev


def _silu(g):
    return g * jax.nn.sigmoid(g)


def _conv_gates(xb, xb_tail, cbw, cbb, w_ri):
    xc = (cbb + cbw[3:4] * xb
          + cbw[2:3] * _shift_rows(xb, xb_tail, 1)
          + cbw[1:2] * _shift_rows(xb, xb_tail, 2)
          + cbw[0:1] * _shift_rows(xb, xb_tail, 3))
    xc_bf = xc.astype(jnp.bfloat16)
    r_parts, i_parts = [], []
    for hh in range(xb.shape[1] // HEAD_DIM):
        ri = _dot(xc_bf[:, hh * HEAD_DIM:(hh + 1) * HEAD_DIM], w_ri[hh])
        r_parts.append(ri[:, :HEAD_DIM])
        i_parts.append(ri[:, HEAD_DIM:])
    return xc, jnp.concatenate(r_parts, axis=1), jnp.concatenate(i_parts, axis=1)


def _group_a_piece(v_a, b_a, c_a, g_a, caw, cab, cv_tail):
    cv = c_a * v_a
    conv = (cab + caw[2:3] * cv
            + caw[1:2] * _shift_rows(cv, cv_tail, 1)
            + caw[0:1] * _shift_rows(cv, cv_tail, 2))
    return b_a * conv * _silu(g_a), cv[cv.shape[0] - SUBLANES:]


def _group_b_piece(xc, r_pre, i_pre, g_b, b_r, b_i, lam_scale, h_prev, is_sequence_start):
    r = jax.nn.sigmoid(r_pre + b_r)
    i_gate = jax.nn.sigmoid(i_pre + b_i)
    log_a = lam_scale * r
    a = jnp.exp(log_a)
    th = jnp.tanh(log_a)
    mult = jnp.sqrt(-2.0 * th / (1.0 - th))
    if is_sequence_start is not None:
        first_row = (lax.broadcasted_iota(jnp.int32, mult.shape, 0) == 0) & is_sequence_start
        mult = jnp.where(first_row, 1.0, mult)
    h, h_prev = _linear_scan(a, mult * i_gate * xc, h_prev)
    return h * _silu(g_b), h_prev


def _mixer_kernel(x_ref, g_ref, wv_ref, wb_ref, wc_ref, wga_ref, wxb_ref, wgb_ref,
                  caw_ref, cab_ref, cbw_ref, cbb_ref, wr_ref, wi_ref, br_ref, bi_ref, lam_ref,
                  wout_ref, wq_ref, wo_ref,
                  ya_ref, yb_ref, wout_bf_ref, wq_bf_ref, wo_bf_ref,
                  win_bf, cv_tail, xb_tail, h_carry):
    p, i = pl.program_id(0), pl.program_id(1)
    t_rows = ya_ref.shape[0]
    bf16 = jnp.bfloat16

    def cast_riders():
        for src, dst in ((wout_ref, wout_bf_ref), (wq_ref, wq_bf_ref), (wo_ref, wo_bf_ref)):
            dst[...] = src[...].astype(bf16)

    def cast_w_in_piece():
        piece = wv_ref.shape[0]
        w_next = win_bf.at[p % 2]
        for g, w_ref in enumerate((wv_ref, wb_ref, wc_ref, wga_ref, wxb_ref, wgb_ref)):
            w_next[g, pl.ds(i * piece, piece), :] = w_ref[...].astype(bf16)

    @pl.when(p == 0)
    def _():
        cast_riders()
        cast_w_in_piece()

    @pl.when(p > 0)
    def _():
        @pl.when(i == 0)
        def _():
            cv_tail[...] = jnp.zeros_like(cv_tail)
            xb_tail[...] = jnp.zeros_like(xb_tail)
            h_carry[...] = jnp.zeros_like(h_carry)

        cast_riders()
        neg_lam = -lam_ref[...]
        softplus_neg_lam = jnp.maximum(neg_lam, 0.0) + jnp.log1p(jnp.exp(-jnp.abs(neg_lam)))
        lam_scale = -LRU_C * softplus_neg_lam
        w_cur = win_bf.at[(p - 1) % 2]
        w_in = [w_cur.at[g] for g in range(6)]
        w_ri = [jnp.concatenate([wr_ref[hh].astype(bf16), wi_ref[hh].astype(bf16)], axis=1)
                for hh in range(wr_ref.shape[0])]
        chunk = MIX_ROW_CHUNK
        assert t_rows == 2 * chunk
        n_split = MIX_CH_TILE // MXU_COLS
        n_sub = chunk // MIX_SUB_ROWS
        a_rows = chunk // (len(_RECURRENCE_GROUPS) * n_split)
        never = p < 0
        cv_t, xb_t, h_prev = cv_tail[...], xb_tail[...], h_carry[...]

        def normed(c):
            return _rms_norm(x_ref[pl.ds(c * chunk, chunk), :], g_ref[...]).astype(bf16)

        def half(hn, k, hf):
            return _dot(hn, w_in[k][:, pl.ds(hf * MXU_COLS, MXU_COLS)])

        def tied(carry, part):
            rows = part[chunk - SUBLANES:]
            return jnp.where(never, jnp.concatenate([rows] * n_split, axis=1), carry)

        def gated_conv(c, parts, hf, cv_half, pieces, under):
            lanes = slice(hf * MXU_COLS, (hf + 1) * MXU_COLS)
            for q in pieces:
                sl = slice(q * a_rows, (q + 1) * a_rows)
                part = under()
                v_a, b_a, c_a, g_a = (parts[k][hf][sl] for k in (_V, _B, _C, _GA))
                y_a, cv_half = _group_a_piece(v_a, b_a, c_a, g_a, caw_ref[:, lanes],
                                              cab_ref[:, lanes], cv_half)
                ya_ref[pl.ds(c * chunk + sl.start, a_rows), lanes] = y_a.astype(ya_ref.dtype)
                if part is not None:
                    cv_half = jnp.where(never, part[chunk - SUBLANES:], cv_half)
            return cv_half

        def recurrence(c, hn, xb, g_b, xb_tail_c, h_prev, cv_half, first):
            xc, r_pre, i_pre = _conv_gates(xb, xb_tail_c, cbw_ref[...], cbb_ref[...], w_ri)
            todo = [(k, hf) for hf in range(n_split) for k in _CONV_GROUPS]
            parts = {k: [None] * n_split for k in _CONV_GROUPS}
            n_first = len(_CONV_GROUPS)
            for s in range(n_sub):
                sl = slice(s * MIX_SUB_ROWS, (s + 1) * MIX_SUB_ROWS)
                k, hf = todo.pop(0)
                parts[k][hf] = half(hn, k, hf)
                y_b, h_prev = _group_b_piece(xc[sl], r_pre[sl], i_pre[sl], g_b[sl], br_ref[...],
                                             bi_ref[...], lam_scale, h_prev, first if s == 0 else None)
                yb_ref[pl.ds(c * chunk + sl.start, MIX_SUB_ROWS), :] = y_b.astype(yb_ref.dtype)
                if s >= n_first:
                    share = (chunk // a_rows) // (n_sub - n_first)
                    done = (s - n_first) * share
                    cv_half = gated_conv(c, parts, 0, cv_half, range(done, done + share), lambda: None)
                h_prev = tied(h_prev, parts[k][hf])
            assert not todo
            return parts, h_prev, cv_half

        assert n_split == 2
        cv_halves = [cv_t[:, :MXU_COLS], cv_t[:, MXU_COLS:]]
        hn0, hn1 = normed(0), normed(1)
        xb0, gb0 = _dot(hn0, w_in[_XB][...]), _dot(hn0, w_in[_GB][...])
        parts0, h_prev, cv_halves[0] = recurrence(0, hn0, xb0, gb0, xb_t, h_prev, cv_halves[0], i == 0)

        todo1 = [(k, hf) for k in _RECURRENCE_GROUPS for hf in range(n_split)]
        parts1 = {k: [None] * n_split for k in _RECURRENCE_GROUPS}

        def next_recurrence_half():
            k, hf = todo1.pop(0)
            parts1[k][hf] = half(hn1, k, hf)
            return parts1[k][hf]

        all_pieces = range(chunk // a_rows)
        cv_halves[1] = gated_conv(0, parts0, 1, cv_halves[1], all_pieces, next_recurrence_half)
        assert not todo1
        xb1, gb1 = (jnp.concatenate(parts1[k], axis=1) for k in _RECURRENCE_GROUPS)
        parts1a, h_prev, cv_halves[0] = recurrence(1, hn1, xb1, gb1, xb0[chunk - SUBLANES:], h_prev,
                                                   cv_halves[0], None)
        cv_halves[1] = gated_conv(1, parts1a, 1, cv_halves[1], all_pieces, lambda: None)
        cv_tail[...] = jnp.concatenate(cv_halves, axis=1)
        xb_tail[...], h_carry[...] = xb1[chunk - SUBLANES:], h_prev
        cast_w_in_piece()


def _attn_kernel(x_ref, ya_ref, yb_ref, woa_ref, wob_ref, gx_ref, wq_ref, k_ref, v_ref,
                 wo_ref, gf_ref, out_ref):
    h = x_ref[...] + _dot(ya_ref[...], woa_ref[...]) + _dot(yb_ref[...], wob_ref[...])
    hn = _rms_norm(h, gx_ref[...]).astype(jnp.bfloat16)
    q = _dot(hn, wq_ref[...]).astype(jnp.bfloat16)
    scale = XATTN_HEAD_DIM ** -0.5
    heads = [slice(hd * XATTN_HEAD_DIM, (hd + 1) * XATTN_HEAD_DIM) for hd in range(XATTN_HEADS)]
    scores = [lax.dot_general(q[:, sl], k_ref[:, sl], (((1,), (1,)), ((), ())),
                              preferred_element_type=jnp.float32) * scale for sl in heads]
    probs = []
    for s in scores:
        p = jnp.exp(s - jnp.max(s, axis=-1, keepdims=True))
        probs.append((p / jnp.sum(p, axis=-1, keepdims=True)).astype(jnp.bfloat16))
    o_parts = [_dot(p, v_ref[:, sl]) for p, sl in zip(probs, heads)]
    o = jnp.concatenate(o_parts, axis=1).astype(jnp.bfloat16)
    h2 = h + _dot(o, wo_ref[...])
    out_ref[...] = _rms_norm(h2, gf_ref[...]).astype(out_ref.dtype)


def _resident(shape):
    return pl.BlockSpec(shape, lambda *_: (0,) * len(shape), pipeline_mode=pl.Buffered(1))


def _layer(h, mem, norm_mix_g, w_in, conv_a_w, conv_a_b, conv_b_w, conv_b_b, w_rgate, b_rgate,
           w_igate, b_igate, lru_lambda, w_out, norm_x_g, norm_mem_g, w_q, w_kv, w_o, norm_f_g):
    seq, d = h.shape
    n_mem = mem.shape[0]
    bf16 = jnp.bfloat16
    row = lambda v: v.reshape(1, -1)

    kv = pl.pallas_call(
        _kv_kernel,
        grid=(2 * d // KV_COL_TILE,),
        in_specs=[pl.BlockSpec((n_mem, d), lambda j: (0, 0)),
                  pl.BlockSpec((1, d), lambda j: (0, 0)),
                  pl.BlockSpec((d, KV_COL_TILE), lambda j: (0, j))],
        out_specs=pl.BlockSpec((n_mem, KV_COL_TILE), lambda j: (0, j)),
        out_shape=jax.ShapeDtypeStruct((n_mem, 2 * d), bf16),
        compiler_params=pltpu.CompilerParams(dimension_semantics=("arbitrary",),
                                             vmem_limit_bytes=VMEM_LIMIT_BYTES),
        name="kv_proj",
    )(mem, row(norm_mem_g), w_kv)

    tm, cb = MIX_SEQ_TILE, MIX_CH_TILE
    n_cb, n_t = d // cb, seq // tm
    n_steps = n_cb * n_t
    piece = d // n_t
    blk = lambda p: jnp.maximum(p - 1, 0)
    nxt = lambda p: jnp.minimum(p, n_cb - 1)
    step = lambda p, i: jnp.minimum(p * n_t + i, n_steps - 1)
    tile = lambda p, i: jnp.where(p > 0, i, 0)
    col = lambda g: pl.BlockSpec((piece, cb), lambda p, i, g=g: (i, g * n_cb + nxt(p)))
    chan = lambda r: pl.BlockSpec((r, cb), lambda p, i: (0, blk(p)))
    gate = pl.BlockSpec((cb // HEAD_DIM, HEAD_DIM, HEAD_DIM), lambda p, i: (blk(p), 0, 0))
    y_spec = pl.BlockSpec((tm, cb), lambda p, i: (tile(p, i), blk(p)))
    rows_of = lambda w: pl.BlockSpec((w.shape[0] // n_steps, d), lambda p, i: (step(p, i), 0))
    y_a, y_b, w_out_bf, w_q_bf, w_o_bf = pl.pallas_call(
        _mixer_kernel,
        grid=(n_cb + 1, n_t),
        in_specs=[pl.BlockSpec((tm, d), lambda p, i: (tile(p, i), 0)),
                  pl.BlockSpec((1, d), lambda p, i: (0, 0)),
                  col(0), col(1), col(2), col(3), col(4), col(5),
                  chan(conv_a_w.shape[0]), chan(1), chan(conv_b_w.shape[0]), chan(1),
                  gate, gate, chan(1), chan(1), chan(1),
                  rows_of(w_out), rows_of(w_q), rows_of(w_o)],
        out_specs=[y_spec, y_spec, rows_of(w_out), rows_of(w_q), rows_of(w_o)],
        out_shape=[jax.ShapeDtypeStruct((seq, d), bf16), jax.ShapeDtypeStruct((seq, d), bf16),
                   jax.ShapeDtypeStruct(w_out.shape, bf16), jax.ShapeDtypeStruct(w_q.shape, bf16),
                   jax.ShapeDtypeStruct(w_o.shape, bf16)],
        scratch_shapes=[pltpu.VMEM((2, 6, d, cb), bf16),
                        pltpu.VMEM((SUBLANES, cb), jnp.float32),
                        pltpu.VMEM((SUBLANES, cb), jnp.float32),
                        pltpu.VMEM((SUBLANES, cb), jnp.float32)],
        compiler_params=pltpu.CompilerParams(dimension_semantics=("arbitrary", "arbitrary"),
                                             vmem_limit_bytes=VMEM_LIMIT_BYTES),
        name="mixer",
    )(h, row(norm_mix_g), w_in, w_in, w_in, w_in, w_in, w_in,
      conv_a_w, row(conv_a_b), conv_b_w, row(conv_b_b), w_rgate, w_igate,
      row(b_rgate), row(b_igate), row(lru_lambda), w_out, w_q, w_o)

    ta = ATTN_SEQ_TILE
    out = pl.pallas_call(
        _attn_kernel,
        grid=(seq // ta,),
        in_specs=[pl.BlockSpec((ta, d), lambda i: (i, 0)),
                  pl.BlockSpec((ta, d), lambda i: (i, 0)),
                  pl.BlockSpec((ta, d), lambda i: (i, 0)),
                  pl.BlockSpec((d, d), lambda i: (0, 0), pipeline_mode=pl.Buffered(1)),
                  pl.BlockSpec((d, d), lambda i: (1, 0), pipeline_mode=pl.Buffered(1)),
                  _resident((1, d)),
                  _resident((d, d)),
                  pl.BlockSpec((n_mem, d), lambda i: (0, 0), pipeline_mode=pl.Buffered(1)),
                  pl.BlockSpec((n_mem, d), lambda i: (0, 1), pipeline_mode=pl.Buffered(1)),
                  _resident((d, d)),
                  _resident((1, d))],
        out_specs=pl.BlockSpec((ta, d), lambda i: (i, 0)),
        out_shape=jax.ShapeDtypeStruct((seq, d), h.dtype),
        compiler_params=pltpu.CompilerParams(dimension_semantics=("arbitrary",),
                                             vmem_limit_bytes=VMEM_LIMIT_BYTES),
        name="outproj_xattn",
    )(h, y_a, y_b, w_out_bf, w_out_bf, row(norm_x_g), w_q_bf, kv, kv, w_o_bf, row(norm_f_g))
    return out


def kernel(x, mem, norm_mix_g, w_in, conv_a_w, conv_a_b, conv_b_w, conv_b_b, w_rgate, b_rgate,
           w_igate, b_igate, lru_lambda, w_out, norm_x_g, norm_mem_g, w_q, w_kv, w_o, norm_f_g):
    bsz, seq, d = x.shape
    depth = w_in.shape[0]
    assert bsz == 1 and depth == 1 and d == D_MODEL
    out = _layer(x[0], mem[0], norm_mix_g[0], w_in[0], conv_a_w[0], conv_a_b[0], conv_b_w[0],
                 conv_b_b[0], w_rgate[0], b_rgate[0], w_igate[0], b_igate[0], lru_lambda[0],
                 w_out[0], norm_x_g[0], norm_mem_g[0], w_q[0], w_kv[0], w_o[0], norm_f_g)
    return out[None]
```

```python
import jax
import jax.numpy as jnp
from jax import lax
from jax.experimental import pallas as pl
from jax.experimental.pallas import tpu as pltpu

D_MODEL = 2048
HEAD_DIM = 128
LRU_C = 8.0
XATTN_HEADS = 4
XATTN_HEAD_DIM = D_MODEL // XATTN_HEADS
RMS_EPS = 1e-6

SUBLANES = 8
MXU_COLS = 256
VMEM_LIMIT_BYTES = 56 * 1024 * 1024

MIX_SEQ_TILE = 512
MIX_ROW_CHUNK = 256
MIX_SUB_ROWS = 32
_V, _B, _C, _GA, _XB, _GB = range(6)
_RECURRENCE_GROUPS = (_XB, _GB)
_CONV_GROUPS = (_C, _V, _B, _GA)
MIX_CH_TILE = 512
ATTN_SEQ_TILE = 256


def _rms_norm(xf, g):
    ms = jnp.mean(xf * xf, axis=-1, keepdims=True)
    return xf * lax.rsqrt(ms + RMS_EPS) * g


def _dot(a, b):
    return jnp.dot(a, b, preferred_element_type=jnp.float32)


def _group_rows(u):
    return u.reshape(u.shape[0] // SUBLANES, SUBLANES, u.shape[1])


def _shift_rows(u, tail, s):
    rows, ch = u.shape
    rolled = pltpu.roll(_group_rows(u), s, axis=1)
    before = jnp.concatenate([pltpu.roll(tail, s, axis=0)[None], rolled[:-1]], axis=0)
    row = lax.broadcasted_iota(jnp.int32, rolled.shape, 1)
    return jnp.where(row < s, before, rolled).reshape(rows, ch)


def _linear_scan(a, b, h_prev):
    t_rows, ch = a.shape
    a, b = _group_rows(a), _group_rows(b)
    row = lax.broadcasted_iota(jnp.int32, a.shape, 1)
    for s in (1, 2, 4):
        keep = row >= s
        b = a * jnp.where(keep, pltpu.roll(b, s, axis=1), 0.0) + b
        a = a * jnp.where(keep, pltpu.roll(a, s, axis=1), 1.0)
    outs = []
    for g in range(t_rows // SUBLANES):
        h_g = a[g] * h_prev + b[g]
        outs.append(h_g)
        h_prev = jnp.broadcast_to(h_g[SUBLANES - 1:SUBLANES, :], (SUBLANES, ch))
    return jnp.concatenate(outs, axis=0), h_prev


def _silu(g):
    return g * jax.nn.sigmoid(g)


def _conv_gates(xb, xb_tail, cbw, cbb, w_ri):
    xc = (cbb + cbw[3:4] * xb
          + cbw[2:3] * _shift_rows(xb, xb_tail, 1)
          + cbw[1:2] * _shift_rows(xb, xb_tail, 2)
          + cbw[0:1] * _shift_rows(xb, xb_tail, 3))
    xc_bf = xc.astype(jnp.bfloat16)
    r_parts, i_parts = [], []
    for hh in range(xb.shape[1] // HEAD_DIM):
        ri = _dot(xc_bf[:, hh * HEAD_DIM:(hh + 1) * HEAD_DIM], w_ri[hh])
        r_parts.append(ri[:, :HEAD_DIM])
        i_parts.append(ri[:, HEAD_DIM:])
    return xc, jnp.concatenate(r_parts, axis=1), jnp.concatenate(i_parts, axis=1)


def _group_a_piece(v_a, b_a, c_a, g_a, caw, cab, cv_tail):
    cv = c_a * v_a
    conv = (cab + caw[2:3] * cv
            + caw[1:2] * _shift_rows(cv, cv_tail, 1)
            + caw[0:1] * _shift_rows(cv, cv_tail, 2))
    return b_a * conv * _silu(g_a), cv[cv.shape[0] - SUBLANES:]


def _group_b_piece(xc, r_pre, i_pre, g_b, b_r, b_i, lam_scale, h_prev, is_sequence_start):
    r = jax.nn.sigmoid(r_pre + b_r)
    i_gate = jax.nn.sigmoid(i_pre + b_i)
    log_a = lam_scale * r
    a = jnp.exp(log_a)
    th = jnp.tanh(log_a)
    mult = jnp.sqrt(-2.0 * th / (1.0 - th))
    if is_sequence_start is not None:
        first_row = (lax.broadcasted_iota(jnp.int32, mult.shape, 0) == 0) & is_sequence_start
        mult = jnp.where(first_row, 1.0, mult)
    h, h_prev = _linear_scan(a, mult * i_gate * xc, h_prev)
    return h * _silu(g_b), h_prev


def _mixer_kernel(x_ref, g_ref, wv_ref, wb_ref, wc_ref, wga_ref, wxb_ref, wgb_ref,
                  caw_ref, cab_ref, cbw_ref, cbb_ref, wr_ref, wi_ref, br_ref, bi_ref, lam_ref,
                  wout_ref, wq_ref, wo_ref, mem_ref, gm_ref, wkv_ref,
                  ya_ref, yb_ref, wout_bf_ref, wq_bf_ref, wo_bf_ref, kv_ref,
                  win_bf, cv_tail, xb_tail, h_carry):
    p, i = pl.program_id(0), pl.program_id(1)
    t_rows = ya_ref.shape[0]
    bf16 = jnp.bfloat16

    def cast_riders():
        for src, dst in ((wout_ref, wout_bf_ref), (wq_ref, wq_bf_ref), (wo_ref, wo_bf_ref)):
            dst[...] = src[...].astype(bf16)

    def cast_w_in_piece():
        piece = wv_ref.shape[0]
        w_next = win_bf.at[p % 2]
        for g, w_ref in enumerate((wv_ref, wb_ref, wc_ref, wga_ref, wxb_ref, wgb_ref)):
            w_next[g, pl.ds(i * piece, piece), :] = w_ref[...].astype(bf16)

    @pl.when(p == 0)
    def _():
        cast_riders()
        cast_w_in_piece()
        mn = _rms_norm(mem_ref[...], gm_ref[...]).astype(bf16)
        kv_ref[...] = _dot(mn, wkv_ref[...].astype(bf16)).astype(kv_ref.dtype)

    @pl.when(p > 0)
    def _():
        @pl.when(i == 0)
        def _():
            cv_tail[...] = jnp.zeros_like(cv_tail)
            xb_tail[...] = jnp.zeros_like(xb_tail)
            h_carry[...] = jnp.zeros_like(h_carry)

        cast_riders()
        neg_lam = -lam_ref[...]
        softplus_neg_lam = jnp.maximum(neg_lam, 0.0) + jnp.log1p(jnp.exp(-jnp.abs(neg_lam)))
        lam_scale = -LRU_C * softplus_neg_lam
        w_cur = win_bf.at[(p - 1) % 2]
        w_in = [w_cur.at[g] for g in range(6)]
        w_ri = [jnp.concatenate([wr_ref[hh].astype(bf16), wi_ref[hh].astype(bf16)], axis=1)
                for hh in range(wr_ref.shape[0])]
        chunk = MIX_ROW_CHUNK
        assert t_rows == 2 * chunk
        n_split = MIX_CH_TILE // MXU_COLS
        n_sub = chunk // MIX_SUB_ROWS
        a_rows = chunk // (len(_RECURRENCE_GROUPS) * n_split)
        never = p < 0
        cv_t, xb_t, h_prev = cv_tail[...], xb_tail[...], h_carry[...]

        def normed(c):
            return _rms_norm(x_ref[pl.ds(c * chunk, chunk), :], g_ref[...]).astype(bf16)

        def half(hn, k, hf):
            return _dot(hn, w_in[k][:, pl.ds(hf * MXU_COLS, MXU_COLS)])

        def tied(carry, part):
            rows = part[chunk - SUBLANES:]
            return jnp.where(never, jnp.concatenate([rows] * n_split, axis=1), carry)

        def gated_conv(c, parts, hf, cv_half, pieces, under):
            lanes = slice(hf * MXU_COLS, (hf + 1) * MXU_COLS)
            for q in pieces:
                sl = slice(q * a_rows, (q + 1) * a_rows)
                part = under()
                v_a, b_a, c_a, g_a = (parts[k][hf][sl] for k in (_V, _B, _C, _GA))
                y_a, cv_half = _group_a_piece(v_a, b_a, c_a, g_a, caw_ref[:, lanes],
                                              cab_ref[:, lanes], cv_half)
                ya_ref[pl.ds(c * chunk + sl.start, a_rows), lanes] = y_a.astype(ya_ref.dtype)
                if part is not None:
                    cv_half = jnp.where(never, part[chunk - SUBLANES:], cv_half)
            return cv_half

        def recurrence(c, hn, xb, g_b, xb_tail_c, h_prev, cv_half, first):
            xc, r_pre, i_pre = _conv_gates(xb, xb_tail_c, cbw_ref[...], cbb_ref[...], w_ri)
            todo = [(k, hf) for hf in range(n_split) for k in _CONV_GROUPS]
            parts = {k: [None] * n_split for k in _CONV_GROUPS}
            n_first = len(_CONV_GROUPS)
            for s in range(n_sub):
                sl = slice(s * MIX_SUB_ROWS, (s + 1) * MIX_SUB_ROWS)
                k, hf = todo.pop(0)
                parts[k][hf] = half(hn, k, hf)
                y_b, h_prev = _group_b_piece(xc[sl], r_pre[sl], i_pre[sl], g_b[sl], br_ref[...],
                                             bi_ref[...], lam_scale, h_prev, first if s == 0 else None)
                yb_ref[pl.ds(c * chunk + sl.start, MIX_SUB_ROWS), :] = y_b.astype(yb_ref.dtype)
                if s >= n_first:
                    share = (chunk // a_rows) // (n_sub - n_first)
                    done = (s - n_first) * share
                    cv_half = gated_conv(c, parts, 0, cv_half, range(done, done + share), lambda: None)
                h_prev = tied(h_prev, parts[k][hf])
            assert not todo
            return parts, h_prev, cv_half

        assert n_split == 2
        cv_halves = [cv_t[:, :MXU_COLS], cv_t[:, MXU_COLS:]]
        hn0, hn1 = normed(0), normed(1)
        xb0, gb0 = _dot(hn0, w_in[_XB][...]), _dot(hn0, w_in[_GB][...])
        parts0, h_prev, cv_halves[0] = recurrence(0, hn0, xb0, gb0, xb_t, h_prev, cv_halves[0], i == 0)

        todo1 = [(k, hf) for k in _RECURRENCE_GROUPS for hf in range(n_split)]
        parts1 = {k: [None] * n_split for k in _RECURRENCE_GROUPS}

        def next_recurrence_half():
            k, hf = todo1.pop(0)
            parts1[k][hf] = half(hn1, k, hf)
            return parts1[k][hf]

        all_pieces = range(chunk // a_rows)
        cv_halves[1] = gated_conv(0, parts0, 1, cv_halves[1], all_pieces, next_recurrence_half)
        assert not todo1
        xb1, gb1 = (jnp.concatenate(parts1[k], axis=1) for k in _RECURRENCE_GROUPS)
        parts1a, h_prev, cv_halves[0] = recurrence(1, hn1, xb1, gb1, xb0[chunk - SUBLANES:], h_prev,
                                                   cv_halves[0], None)
        cv_halves[1] = gated_conv(1, parts1a, 1, cv_halves[1], all_pieces, lambda: None)
        cv_tail[...] = jnp.concatenate(cv_halves, axis=1)
        xb_tail[...], h_carry[...] = xb1[chunk - SUBLANES:], h_prev
        cast_w_in_piece()


def _attn_kernel(x_ref, ya_ref, yb_ref, woa_ref, wob_ref, gx_ref, wq_ref, k_ref, v_ref,
                 wo_ref, gf_ref, out_ref):
    h = x_ref[...] + _dot(ya_ref[...], woa_ref[...]) + _dot(yb_ref[...], wob_ref[...])
    hn = _rms_norm(h, gx_ref[...]).astype(jnp.bfloat16)
    q = _dot(hn, wq_ref[...]).astype(jnp.bfloat16)
    scale = XATTN_HEAD_DIM ** -0.5
    heads = [slice(hd * XATTN_HEAD_DIM, (hd + 1) * XATTN_HEAD_DIM) for hd in range(XATTN_HEADS)]
    scores = [lax.dot_general(q[:, sl], k_ref[:, sl], (((1,), (1,)), ((), ())),
                              preferred_element_type=jnp.float32) * scale for sl in heads]
    probs = []
    for s in scores:
        p = jnp.exp(s - jnp.max(s, axis=-1, keepdims=True))
        probs.append((p / jnp.sum(p, axis=-1, keepdims=True)).astype(jnp.bfloat16))
    o_parts = [_dot(p, v_ref[:, sl]) for p, sl in zip(probs, heads)]
    o = jnp.concatenate(o_parts, axis=1).astype(jnp.bfloat16)
    h2 = h + _dot(o, wo_ref[...])
    out_ref[...] = _rms_norm(h2, gf_ref[...]).astype(out_ref.dtype)


def _resident(shape):
    return pl.BlockSpec(shape, lambda *_: (0,) * len(shape), pipeline_mode=pl.Buffered(1))


def _layer(h, mem, norm_mix_g, w_in, conv_a_w, conv_a_b, conv_b_w, conv_b_b, w_rgate, b_rgate,
           w_igate, b_igate, lru_lambda, w_out, norm_x_g, norm_mem_g, w_q, w_kv, w_o, norm_f_g):
    seq, d = h.shape
    n_mem = mem.shape[0]
    bf16 = jnp.bfloat16
    row = lambda v: v.reshape(1, -1)

    tm, cb = MIX_SEQ_TILE, MIX_CH_TILE
    n_cb, n_t = d // cb, seq // tm
    n_steps = n_cb * n_t
    piece = d // n_t
    blk = lambda p: jnp.maximum(p - 1, 0)
    nxt = lambda p: jnp.minimum(p, n_cb - 1)
    step = lambda p, i: jnp.minimum(p * n_t + i, n_steps - 1)
    tile = lambda p, i: jnp.where(p > 0, i, 0)
    col = lambda g: pl.BlockSpec((piece, cb), lambda p, i, g=g: (i, g * n_cb + nxt(p)))
    chan = lambda r: pl.BlockSpec((r, cb), lambda p, i: (0, blk(p)))
    gate = pl.BlockSpec((cb // HEAD_DIM, HEAD_DIM, HEAD_DIM), lambda p, i: (blk(p), 0, 0))
    y_spec = pl.BlockSpec((tm, cb), lambda p, i: (tile(p, i), blk(p)))
    rows_of = lambda w: pl.BlockSpec((w.shape[0] // n_steps, d), lambda p, i: (step(p, i), 0))
    kv_cols = 2 * d // n_t
    kv_blk = lambda p, i: (0, jnp.where(p > 0, n_t - 1, i))
    y_a, y_b, w_out_bf, w_q_bf, w_o_bf, kv = pl.pallas_call(
        _mixer_kernel,
        grid=(n_cb + 1, n_t),
        in_specs=[pl.BlockSpec((tm, d), lambda p, i: (tile(p, i), 0)),
                  pl.BlockSpec((1, d), lambda p, i: (0, 0)),
                  col(0), col(1), col(2), col(3), col(4), col(5),
                  chan(conv_a_w.shape[0]), chan(1), chan(conv_b_w.shape[0]), chan(1),
                  gate, gate, chan(1), chan(1), chan(1),
                  rows_of(w_out), rows_of(w_q), rows_of(w_o),
                  _resident((n_mem, d)), _resident((1, d)), pl.BlockSpec((d, kv_cols), kv_blk)],
        out_specs=[y_spec, y_spec, rows_of(w_out), rows_of(w_q), rows_of(w_o),
                   pl.BlockSpec((n_mem, kv_cols), kv_blk)],
        out_shape=[jax.ShapeDtypeStruct((seq, d), bf16), jax.ShapeDtypeStruct((seq, d), bf16),
                   jax.ShapeDtypeStruct(w_out.shape, bf16), jax.ShapeDtypeStruct(w_q.shape, bf16),
                   jax.ShapeDtypeStruct(w_o.shape, bf16), jax.ShapeDtypeStruct((n_mem, 2 * d), bf16)],
        scratch_shapes=[pltpu.VMEM((2, 6, d, cb), bf16),
                        pltpu.VMEM((SUBLANES, cb), jnp.float32),
                        pltpu.VMEM((SUBLANES, cb), jnp.float32),
                        pltpu.VMEM((SUBLANES, cb), jnp.float32)],
        compiler_params=pltpu.CompilerParams(dimension_semantics=("arbitrary", "arbitrary"),
                                             vmem_limit_bytes=VMEM_LIMIT_BYTES),
        name="mixer",
    )(h, row(norm_mix_g), w_in, w_in, w_in, w_in, w_in, w_in,
      conv_a_w, row(conv_a_b), conv_b_w, row(conv_b_b), w_rgate, w_igate,
      row(b_rgate), row(b_igate), row(lru_lambda), w_out, w_q, w_o, mem, row(norm_mem_g), w_kv)

    ta = ATTN_SEQ_TILE
    out = pl.pallas_call(
        _attn_kernel,
        grid=(seq // ta,),
        in_specs=[pl.BlockSpec((ta, d), lambda i: (i, 0)),
                  pl.BlockSpec((ta, d), lambda i: (i, 0)),
                  pl.BlockSpec((ta, d), lambda i: (i, 0)),
                  pl.BlockSpec((d, d), lambda i: (0, 0), pipeline_mode=pl.Buffered(1)),
                  pl.BlockSpec((d, d), lambda i: (1, 0), pipeline_mode=pl.Buffered(1)),
                  _resident((1, d)),
                  _resident((d, d)),
                  pl.BlockSpec((n_mem, d), lambda i: (0, 0), pipeline_mode=pl.Buffered(1)),
                  pl.BlockSpec((n_mem, d), lambda i: (0, 1), pipeline_mode=pl.Buffered(1)),
                  _resident((d, d)),
                  _resident((1, d))],
        out_specs=pl.BlockSpec((ta, d), lambda i: (i, 0)),
        out_shape=jax.ShapeDtypeStruct((seq, d), h.dtype),
        compiler_params=pltpu.CompilerParams(dimension_semantics=("arbitrary",),
                                             vmem_limit_bytes=VMEM_LIMIT_BYTES),
        name="outproj_xattn",
    )(h, y_a, y_b, w_out_bf, w_out_bf, row(norm_x_g), w_q_bf, kv, kv, w_o_bf, row(norm_f_g))
    return out


def kernel(x, mem, norm_mix_g, w_in, conv_a_w, conv_a_b, conv_b_w, conv_b_b, w_rgate, b_rgate,
           w_igate, b_igate, lru_lambda, w_out, norm_x_g, norm_mem_g, w_q, w_kv, w_o, norm_f_g):
    bsz, seq, d = x.shape
    depth = w_in.shape[0]
    assert bsz == 1 and depth == 1 and d == D_MODEL
    out = _layer(x[0], mem[0], norm_mix_g[0], w_in[0], conv_a_w[0], conv_a_b[0], conv_b_w[0],
                 conv_b_b[0], w_rgate[0], b_rgate[0], w_igate[0], b_igate[0], lru_lambda[0],
                 w_out[0], norm_x_g[0], norm_mem_g[0], w_q[0], w_kv[0], w_o[0], norm_f_g)
    return out[None]
```

```python
import jax
import jax.numpy as jnp
from jax import lax
from jax.experimental import pallas as pl
from jax.experimental.pallas import tpu as pltpu

D_MODEL = 2048
HEAD_DIM = 128
LRU_C = 8.0
XATTN_HEADS = 4
XATTN_HEAD_DIM = D_MODEL // XATTN_HEADS
RMS_EPS = 1e-6

SUBLANES = 8
MXU_COLS = 256
VMEM_LIMIT_BYTES = 56 * 1024 * 1024

MIX_SEQ_TILE = 512
MIX_ROW_CHUNK = 256
MIX_SUB_ROWS = 32
_V, _B, _C, _GA, _XB, _GB = range(6)
_RECURRENCE_GROUPS = (_XB, _GB)
_CONV_GROUPS = (_C, _V, _B, _GA)
MIX_CH_TILE = 512
ATTN_SEQ_TILE = 256


def _rms_norm(xf, g):
    ms = jnp.mean(xf * xf, axis=-1, keepdims=True)
    return xf * lax.rsqrt(ms + RMS_EPS) * g


def _dot(a, b):
    return jnp.dot(a, b, preferred_element_type=jnp.float32)


def _group_rows(u):
    return u.reshape(u.shape[0] // SUBLANES, SUBLANES, u.shape[1])


def _shift_rows(u, tail, s):
    rows, ch = u.shape
    rolled = pltpu.roll(_group_rows(u), s, axis=1)
    before = jnp.concatenate([pltpu.roll(tail, s, axis=0)[None], rolled[:-1]], axis=0)
    row = lax.broadcasted_iota(jnp.int32, rolled.shape, 1)
    return jnp.where(row < s, before, rolled).reshape(rows, ch)


def _linear_scan(a, b, h_prev):
    t_rows, ch = a.shape
    a, b = _group_rows(a), _group_rows(b)
    row = lax.broadcasted_iota(jnp.int32, a.shape, 1)
    for s in (1, 2, 4):
        keep = row >= s
        b = a * jnp.where(keep, pltpu.roll(b, s, axis=1), 0.0) + b
        a = a * jnp.where(keep, pltpu.roll(a, s, axis=1), 1.0)
    outs = []
    for g in range(t_rows // SUBLANES):
        h_g = a[g] * h_prev + b[g]
        outs.append(h_g)
        h_prev = jnp.broadcast_to(h_g[SUBLANES - 1:SUBLANES, :], (SUBLANES, ch))
    return jnp.concatenate(outs, axis=0), h_prev


def _silu(g):
    return g * jax.nn.sigmoid(g)


def _conv_gates(xb, xb_tail, cbw, cbb, w_ri):
    xc = (cbb + cbw[3:4] * xb
          + cbw[2:3] * _shift_rows(xb, xb_tail, 1)
          + cbw[1:2] * _shift_rows(xb, xb_tail, 2)
          + cbw[0:1] * _shift_rows(xb, xb_tail, 3))
    xc_bf = xc.astype(jnp.bfloat16)
    r_parts, i_parts = [], []
    for hh in range(xb.shape[1] // HEAD_DIM):
        ri = _dot(xc_bf[:, hh * HEAD_DIM:(hh + 1) * HEAD_DIM], w_ri[hh])
        r_parts.append(ri[:, :HEAD_DIM])
        i_parts.append(ri[:, HEAD_DIM:])
    return xc, jnp.concatenate(r_parts, axis=1), jnp.concatenate(i_parts, axis=1)


def _group_a_piece(v_a, b_a, c_a, g_a, caw, cab, cv_tail):
    cv = c_a * v_a
    conv = (cab + caw[2:3] * cv
            + caw[1:2] * _shift_rows(cv, cv_tail, 1)
            + caw[0:1] * _shift_rows(cv, cv_tail, 2))
    return b_a * conv * _silu(g_a), cv[cv.shape[0] - SUBLANES:]


def _group_b_piece(xc, r_pre, i_pre, g_b, b_r, b_i, lam_scale, h_prev, is_sequence_start):
    r = jax.nn.sigmoid(r_pre + b_r)
    i_gate = jax.nn.sigmoid(i_pre + b_i)
    log_a = lam_scale * r
    a = jnp.exp(log_a)
    th = jnp.tanh(log_a)
    mult = jnp.sqrt(-2.0 * th / (1.0 - th))
    if is_sequence_start is not None:
        first_row = (lax.broadcasted_iota(jnp.int32, mult.shape, 0) == 0) & is_sequence_start
        mult = jnp.where(first_row, 1.0, mult)
    h, h_prev = _linear_scan(a, mult * i_gate * xc, h_prev)
    return h * _silu(g_b), h_prev


def _mixer_kernel(x_ref, xn_ref, g_ref, wv_ref, wb_ref, wc_ref, wga_ref, wxb_ref, wgb_ref,
                  caw_ref, cab_ref, cbw_ref, cbb_ref, wr_ref, wi_ref, br_ref, bi_ref, lam_ref,
                  wout_ref, wq_ref, wo_ref, mem_ref, gm_ref, wkv_ref,
                  ya_ref, yb_ref, wout_bf_ref, wq_bf_ref, wo_bf_ref, kv_ref,
                  win_bf, cv_tail, xb_tail, h_carry, hn_next):
    p, i = pl.program_id(0), pl.program_id(1)
    t_rows = ya_ref.shape[0]
    bf16 = jnp.bfloat16

    def cast_riders():
        for src, dst in ((wout_ref, wout_bf_ref), (wq_ref, wq_bf_ref), (wo_ref, wo_bf_ref)):
            dst[...] = src[...].astype(bf16)

    def cast_w_in_piece():
        piece = wv_ref.shape[0]
        w_next = win_bf.at[p % 2]
        for g, w_ref in enumerate((wv_ref, wb_ref, wc_ref, wga_ref, wxb_ref, wgb_ref)):
            w_next[g, pl.ds(i * piece, piece), :] = w_ref[...].astype(bf16)

    @pl.when(p == 0)
    def _():
        cast_riders()
        cast_w_in_piece()
        mn = _rms_norm(mem_ref[...], gm_ref[...]).astype(bf16)
        kv_ref[...] = _dot(mn, wkv_ref[...].astype(bf16)).astype(kv_ref.dtype)

        @pl.when(i == pl.num_programs(1) - 1)
        def _():
            hn_next[...] = _rms_norm(xn_ref[...], g_ref[...]).astype(bf16)

    @pl.when(p > 0)
    def _():
        @pl.when(i == 0)
        def _():
            cv_tail[...] = jnp.zeros_like(cv_tail)
            xb_tail[...] = jnp.zeros_like(xb_tail)
            h_carry[...] = jnp.zeros_like(h_carry)

        cast_riders()
        neg_lam = -lam_ref[...]
        softplus_neg_lam = jnp.maximum(neg_lam, 0.0) + jnp.log1p(jnp.exp(-jnp.abs(neg_lam)))
        lam_scale = -LRU_C * softplus_neg_lam
        w_cur = win_bf.at[(p - 1) % 2]
        w_in = [w_cur.at[g] for g in range(6)]
        w_ri = [jnp.concatenate([wr_ref[hh].astype(bf16), wi_ref[hh].astype(bf16)], axis=1)
                for hh in range(wr_ref.shape[0])]
        chunk = MIX_ROW_CHUNK
        assert t_rows == 2 * chunk
        n_split = MIX_CH_TILE // MXU_COLS
        n_sub = chunk // MIX_SUB_ROWS
        a_rows = chunk // (len(_RECURRENCE_GROUPS) * n_split)
        never = p < 0
        cv_t, xb_t, h_prev = cv_tail[...], xb_tail[...], h_carry[...]

        def half(hn, k, hf):
            return _dot(hn[...], w_in[k][:, pl.ds(hf * MXU_COLS, MXU_COLS)])

        def tied(carry, part):
            rows = part[chunk - SUBLANES:]
            return jnp.where(never, jnp.concatenate([rows] * n_split, axis=1), carry)

        def gated_conv(c, parts, hf, cv_half, pieces, under):
            lanes = slice(hf * MXU_COLS, (hf + 1) * MXU_COLS)
            for q in pieces:
                sl = slice(q * a_rows, (q + 1) * a_rows)
                ties = under()
                v_a, b_a, c_a, g_a = (parts[k][hf][sl] for k in (_V, _B, _C, _GA))
                y_a, cv_half = _group_a_piece(v_a, b_a, c_a, g_a, caw_ref[:, lanes],
                                              cab_ref[:, lanes], cv_half)
                ya_ref[pl.ds(c * chunk + sl.start, a_rows), lanes] = y_a.astype(ya_ref.dtype)
                for rows in ties:
                    cv_half = jnp.where(never, rows, cv_half)
            return cv_half

        def recurrence(c, hn, xb, g_b, xb_tail_c, h_prev, cv_half, first):
            xc, r_pre, i_pre = _conv_gates(xb, xb_tail_c, cbw_ref[...], cbb_ref[...], w_ri)
            todo = [(k, hf) for hf in range(n_split) for k in _CONV_GROUPS]
            parts = {k: [None] * n_split for k in _CONV_GROUPS}
            n_first = len(_CONV_GROUPS)
            for s in range(n_sub):
                sl = slice(s * MIX_SUB_ROWS, (s + 1) * MIX_SUB_ROWS)
                k, hf = todo.pop(0)
                parts[k][hf] = half(hn, k, hf)
                y_b, h_prev = _group_b_piece(xc[sl], r_pre[sl], i_pre[sl], g_b[sl], br_ref[...],
                                             bi_ref[...], lam_scale, h_prev, first if s == 0 else None)
                yb_ref[pl.ds(c * chunk + sl.start, MIX_SUB_ROWS), :] = y_b.astype(yb_ref.dtype)
                if s >= n_first:
                    share = (chunk // a_rows) // (n_sub - n_first)
                    done = (s - n_first) * share
                    cv_half = gated_conv(c, parts, 0, cv_half, range(done, done + share), lambda: ())
                h_prev = tied(h_prev, parts[k][hf])
            assert not todo
            return parts, h_prev, cv_half

        assert n_split == 2
        cv_halves = [cv_t[:, :MXU_COLS], cv_t[:, MXU_COLS:]]
        hn0 = hn_next
        hn1 = _rms_norm(x_ref[pl.ds(chunk, chunk), :], g_ref[...]).astype(bf16)
        xb0, gb0 = _dot(hn0[...], w_in[_XB][...]), _dot(hn0[...], w_in[_GB][...])
        parts0, h_prev, cv_halves[0] = recurrence(0, hn0, xb0, gb0, xb_t, h_prev, cv_halves[0], i == 0)

        todo1 = [(k, hf) for k in _RECURRENCE_GROUPS for hf in range(n_split)]
        parts1 = {k: [None] * n_split for k in _RECURRENCE_GROUPS}

        all_pieces = range(chunk // a_rows)
        assert len(todo1) == len(all_pieces)

        def next_recurrence_half():
            q = len(all_pieces) - len(todo1)
            k, hf = todo1.pop(0)
            parts1[k][hf] = half(hn1, k, hf)
            rows = pl.ds(q * a_rows, a_rows)
            hn = _rms_norm(xn_ref[rows, :], g_ref[...])
            hn_next[rows, :] = hn.astype(bf16)
            return parts1[k][hf][chunk - SUBLANES:], hn[a_rows - SUBLANES:, :MXU_COLS]

        cv_halves[1] = gated_conv(0, parts0, 1, cv_halves[1], all_pieces, next_recurrence_half)
        assert not todo1
        xb1, gb1 = (jnp.concatenate(parts1[k], axis=1) for k in _RECURRENCE_GROUPS)
        parts1a, h_prev, cv_halves[0] = recurrence(1, hn1, xb1, gb1, xb0[chunk - SUBLANES:], h_prev,
                                                   cv_halves[0], None)
        cv_halves[1] = gated_conv(1, parts1a, 1, cv_halves[1], all_pieces, lambda: ())
        cv_tail[...] = jnp.concatenate(cv_halves, axis=1)
        xb_tail[...], h_carry[...] = xb1[chunk - SUBLANES:], h_prev
        cast_w_in_piece()


def _attn_kernel(x_ref, ya_ref, yb_ref, woa_ref, wob_ref, gx_ref, wq_ref, k_ref, v_ref,
                 wo_ref, gf_ref, out_ref):
    h = x_ref[...] + _dot(ya_ref[...], woa_ref[...]) + _dot(yb_ref[...], wob_ref[...])
    hn = _rms_norm(h, gx_ref[...]).astype(jnp.bfloat16)
    q = _dot(hn, wq_ref[...]).astype(jnp.bfloat16)
    scale = XATTN_HEAD_DIM ** -0.5
    heads = [slice(hd * XATTN_HEAD_DIM, (hd + 1) * XATTN_HEAD_DIM) for hd in range(XATTN_HEADS)]
    scores = [lax.dot_general(q[:, sl], k_ref[:, sl], (((1,), (1,)), ((), ())),
                              preferred_element_type=jnp.float32) * scale for sl in heads]
    probs = []
    for s in scores:
        p = jnp.exp(s - jnp.max(s, axis=-1, keepdims=True))
        probs.append((p / jnp.sum(p, axis=-1, keepdims=True)).astype(jnp.bfloat16))
    o_parts = [_dot(p, v_ref[:, sl]) for p, sl in zip(probs, heads)]
    o = jnp.concatenate(o_parts, axis=1).astype(jnp.bfloat16)
    h2 = h + _dot(o, wo_ref[...])
    out_ref[...] = _rms_norm(h2, gf_ref[...]).astype(out_ref.dtype)


def _resident(shape):
    return pl.BlockSpec(shape, lambda *_: (0,) * len(shape), pipeline_mode=pl.Buffered(1))


def _layer(h, mem, norm_mix_g, w_in, conv_a_w, conv_a_b, conv_b_w, conv_b_b, w_rgate, b_rgate,
           w_igate, b_igate, lru_lambda, w_out, norm_x_g, norm_mem_g, w_q, w_kv, w_o, norm_f_g):
    seq, d = h.shape
    n_mem = mem.shape[0]
    bf16 = jnp.bfloat16
    row = lambda v: v.reshape(1, -1)

    tm, cb = MIX_SEQ_TILE, MIX_CH_TILE
    n_cb, n_t = d // cb, seq // tm
    n_steps = n_cb * n_t
    piece = d // n_t
    blk = lambda p: jnp.maximum(p - 1, 0)
    nxt = lambda p: jnp.minimum(p, n_cb - 1)
    step = lambda p, i: jnp.minimum(p * n_t + i, n_steps - 1)
    tile = lambda p, i: jnp.where(p > 0, i, 0)
    next_chunk = lambda p, i: jnp.where(p > 0, (i + 1) % n_t, 0) * (tm // MIX_ROW_CHUNK)
    col = lambda g: pl.BlockSpec((piece, cb), lambda p, i, g=g: (i, g * n_cb + nxt(p)))
    chan = lambda r: pl.BlockSpec((r, cb), lambda p, i: (0, blk(p)))
    gate = pl.BlockSpec((cb // HEAD_DIM, HEAD_DIM, HEAD_DIM), lambda p, i: (blk(p), 0, 0))
    y_spec = pl.BlockSpec((tm, cb), lambda p, i: (tile(p, i), blk(p)))
    rows_of = lambda w: pl.BlockSpec((w.shape[0] // n_steps, d), lambda p, i: (step(p, i), 0))
    kv_cols = 2 * d // n_t
    kv_blk = lambda p, i: (0, jnp.where(p > 0, n_t - 1, i))
    y_a, y_b, w_out_bf, w_q_bf, w_o_bf, kv = pl.pallas_call(
        _mixer_kernel,
        grid=(n_cb + 1, n_t),
        in_specs=[pl.BlockSpec((tm, d), lambda p, i: (tile(p, i), 0)),
                  pl.BlockSpec((MIX_ROW_CHUNK, d), lambda p, i: (next_chunk(p, i), 0)),
                  pl.BlockSpec((1, d), lambda p, i: (0, 0)),
                  col(0), col(1), col(2), col(3), col(4), col(5),
                  chan(conv_a_w.shape[0]), chan(1), chan(conv_b_w.shape[0]), chan(1),
                  gate, gate, chan(1), chan(1), chan(1),
                  rows_of(w_out), rows_of(w_q), rows_of(w_o),
                  _resident((n_mem, d)), _resident((1, d)), pl.BlockSpec((d, kv_cols), kv_blk)],
        out_specs=[y_spec, y_spec, rows_of(w_out), rows_of(w_q), rows_of(w_o),
                   pl.BlockSpec((n_mem, kv_cols), kv_blk)],
        out_shape=[jax.ShapeDtypeStruct((seq, d), bf16), jax.ShapeDtypeStruct((seq, d), bf16),
                   jax.ShapeDtypeStruct(w_out.shape, bf16), jax.ShapeDtypeStruct(w_q.shape, bf16),
                   jax.ShapeDtypeStruct(w_o.shape, bf16), jax.ShapeDtypeStruct((n_mem, 2 * d), bf16)],
        scratch_shapes=[pltpu.VMEM((2, 6, d, cb), bf16),
                        pltpu.VMEM((SUBLANES, cb), jnp.float32),
                        pltpu.VMEM((SUBLANES, cb), jnp.float32),
                        pltpu.VMEM((SUBLANES, cb), jnp.float32),
                        pltpu.VMEM((MIX_ROW_CHUNK, d), bf16)],
        compiler_params=pltpu.CompilerParams(dimension_semantics=("arbitrary", "arbitrary"),
                                             vmem_limit_bytes=VMEM_LIMIT_BYTES),
        name="mixer",
    )(h, h, row(norm_mix_g), w_in, w_in, w_in, w_in, w_in, w_in,
      conv_a_w, row(conv_a_b), conv_b_w, row(conv_b_b), w_rgate, w_igate,
      row(b_rgate), row(b_igate), row(lru_lambda), w_out, w_q, w_o, mem, row(norm_mem_g), w_kv)

    ta = ATTN_SEQ_TILE
    out = pl.pallas_call(
        _attn_kernel,
        grid=(seq // ta,),
        in_specs=[pl.BlockSpec((ta, d), lambda i: (i, 0)),
                  pl.BlockSpec((ta, d), lambda i: (i, 0)),
                  pl.BlockSpec((ta, d), lambda i: (i, 0)),
                  pl.BlockSpec((d, d), lambda i: (0, 0), pipeline_mode=pl.Buffered(1)),
                  pl.BlockSpec((d, d), lambda i: (1, 0), pipeline_mode=pl.Buffered(1)),
                  _resident((1, d)),
                  _resident((d, d)),
                  pl.BlockSpec((n_mem, d), lambda i: (0, 0), pipeline_mode=pl.Buffered(1)),
                  pl.BlockSpec((n_mem, d), lambda i: (0, 1), pipeline_mode=pl.Buffered(1)),
                  _resident((d, d)),
                  _resident((1, d))],
        out_specs=pl.BlockSpec((ta, d), lambda i: (i, 0)),
        out_shape=jax.ShapeDtypeStruct((seq, d), h.dtype),
        compiler_params=pltpu.CompilerParams(dimension_semantics=("arbitrary",),
                                             vmem_limit_bytes=VMEM_LIMIT_BYTES),
        name="outproj_xattn",
    )(h, y_a, y_b, w_out_bf, w_out_bf, row(norm_x_g), w_q_bf, kv, kv, w_o_bf, row(norm_f_g))
    return out


def kernel(x, mem, norm_mix_g, w_in, conv_a_w, conv_a_b, conv_b_w, conv_b_b, w_rgate, b_rgate,
           w_igate, b_igate, lru_lambda, w_out, norm_x_g, norm_mem_g, w_q, w_kv, w_o, norm_f_g):
    bsz, seq, d = x.shape
    depth = w_in.shape[0]
    assert bsz == 1 and depth == 1 and d == D_MODEL
    out = _layer(x[0], mem[0], norm_mix_g[0], w_in[0], conv_a_w[0], conv_a_b[0], conv_b_w[0],
                 conv_b_b[0], w_rgate[0], b_rgate[0], w_igate[0], b_igate[0], lru_lambda[0],
                 w_out[0], norm_x_g[0], norm_mem_g[0], w_q[0], w_kv[0], w_o[0], norm_f_g)
    return out[None]
```

```python
import jax
import jax.numpy as jnp
from jax import lax
from jax.experimental import pallas as pl
from jax.experimental.pallas import tpu as pltpu

D_MODEL = 2048
HEAD_DIM = 128
LRU_C = 8.0
XATTN_HEADS = 4
XATTN_HEAD_DIM = D_MODEL // XATTN_HEADS
RMS_EPS = 1e-6

SUBLANES = 8
MXU_COLS = 256
VMEM_LIMIT_BYTES = 56 * 1024 * 1024

MIX_SEQ_TILE = 512
MIX_ROW_CHUNK = 256
MIX_SUB_ROWS = 32
_V, _B, _C, _GA, _XB, _GB = range(6)
_RECURRENCE_GROUPS = (_XB, _GB)
_CONV_GROUPS = (_C, _V, _B, _GA)
MIX_CH_TILE = 512
ATTN_SEQ_TILE = 256


def _rms_norm(xf, g):
    ms = jnp.mean(xf * xf, axis=-1, keepdims=True)
    return xf * lax.rsqrt(ms + RMS_EPS) * g


def _dot(a, b):
    return jnp.dot(a, b, preferred_element_type=jnp.float32)


def _group_rows(u):
    return u.reshape(u.shape[0] // SUBLANES, SUBLANES, u.shape[1])


def _shift_rows(u, tail, s):
    rows, ch = u.shape
    rolled = pltpu.roll(_group_rows(u), s, axis=1)
    before = jnp.concatenate([pltpu.roll(tail, s, axis=0)[None], rolled[:-1]], axis=0)
    row = lax.broadcasted_iota(jnp.int32, rolled.shape, 1)
    return jnp.where(row < s, before, rolled).reshape(rows, ch)


def _linear_scan(a, b, h_prev):
    t_rows, ch = a.shape
    a, b = _group_rows(a), _group_rows(b)
    row = lax.broadcasted_iota(jnp.int32, a.shape, 1)
    for s in (1, 2, 4):
        keep = row >= s
        b = a * jnp.where(keep, pltpu.roll(b, s, axis=1), 0.0) + b
        a = a * jnp.where(keep, pltpu.roll(a, s, axis=1), 1.0)
    outs = []
    for g in range(t_rows // SUBLANES):
        h_g = a[g] * h_prev + b[g]
        outs.append(h_g)
        h_prev = jnp.broadcast_to(h_g[SUBLANES - 1:SUBLANES, :], (SUBLANES, ch))
    return jnp.concatenate(outs, axis=0), h_prev


def _silu(g):
    return g * jax.nn.sigmoid(g)


def _conv_gates(xb, xb_tail, cbw, cbb, w_ri):
    xc = (cbb + cbw[3:4] * xb
          + cbw[2:3] * _shift_rows(xb, xb_tail, 1)
          + cbw[1:2] * _shift_rows(xb, xb_tail, 2)
          + cbw[0:1] * _shift_rows(xb, xb_tail, 3))
    xc_bf = xc.astype(jnp.bfloat16)
    r_parts, i_parts = [], []
    for hh in range(xb.shape[1] // HEAD_DIM):
        ri = _dot(xc_bf[:, hh * HEAD_DIM:(hh + 1) * HEAD_DIM], w_ri[hh])
        r_parts.append(ri[:, :HEAD_DIM])
        i_parts.append(ri[:, HEAD_DIM:])
    return xc, jnp.concatenate(r_parts, axis=1), jnp.concatenate(i_parts, axis=1)


def _group_a_piece(v_a, b_a, c_a, g_a, caw, cab, cv_tail):
    cv = c_a * v_a
    conv = (cab + caw[2:3] * cv
            + caw[1:2] * _shift_rows(cv, cv_tail, 1)
            + caw[0:1] * _shift_rows(cv, cv_tail, 2))
    return b_a * conv * _silu(g_a), cv[cv.shape[0] - SUBLANES:]


def _group_b_piece(xc, r_pre, i_pre, g_b, b_r, b_i, lam_scale, h_prev, is_sequence_start):
    r = jax.nn.sigmoid(r_pre + b_r)
    i_gate = jax.nn.sigmoid(i_pre + b_i)
    log_a = lam_scale * r
    a = jnp.exp(log_a)
    th = jnp.tanh(log_a)
    mult = jnp.sqrt(-2.0 * th / (1.0 - th))
    if is_sequence_start is not None:
        first_row = (lax.broadcasted_iota(jnp.int32, mult.shape, 0) == 0) & is_sequence_start
        mult = jnp.where(first_row, 1.0, mult)
    h, h_prev = _linear_scan(a, mult * i_gate * xc, h_prev)
    return h * _silu(g_b), h_prev


def _mixer_kernel(x_ref, g_ref, wv_ref, wb_ref, wc_ref, wga_ref, wxb_ref, wgb_ref,
                  caw_ref, cab_ref, cbw_ref, cbb_ref, wr_ref, wi_ref, br_ref, bi_ref, lam_ref,
                  wout_ref, wq_ref, wo_ref, mem_ref, gm_ref, wkv_ref,
                  ya_ref, yb_ref, wout_bf_ref, wq_bf_ref, wo_bf_ref, kv_ref,
                  win_bf, cv_tail, xb_tail, h_carry):
    p, i = pl.program_id(0), pl.program_id(1)
    t_rows = ya_ref.shape[0]
    bf16 = jnp.bfloat16

    def cast_riders():
        for src, dst in ((wout_ref, wout_bf_ref), (wq_ref, wq_bf_ref), (wo_ref, wo_bf_ref)):
            dst[...] = src[...].astype(bf16)

    def cast_w_in_piece():
        piece = wv_ref.shape[0]
        w_next = win_bf.at[p % 2]
        for g, w_ref in enumerate((wv_ref, wb_ref, wc_ref, wga_ref, wxb_ref, wgb_ref)):
            w_next[g, pl.ds(i * piece, piece), :] = w_ref[...].astype(bf16)

    @pl.when(p == 0)
    def _():
        cast_riders()
        cast_w_in_piece()
        mn = _rms_norm(mem_ref[...], gm_ref[...]).astype(bf16)
        kv_ref[...] = _dot(mn, wkv_ref[...].astype(bf16)).astype(kv_ref.dtype)

    @pl.when(p > 0)
    def _():
        @pl.when(i == 0)
        def _():
            cv_tail[...] = jnp.zeros_like(cv_tail)
            xb_tail[...] = jnp.zeros_like(xb_tail)
            h_carry[...] = jnp.zeros_like(h_carry)

        cast_riders()
        neg_lam = -lam_ref[...]
        softplus_neg_lam = jnp.maximum(neg_lam, 0.0) + jnp.log1p(jnp.exp(-jnp.abs(neg_lam)))
        lam_scale = -LRU_C * softplus_neg_lam
        w_cur = win_bf.at[(p - 1) % 2]
        w_in = [w_cur.at[g] for g in range(6)]
        w_ri = [jnp.concatenate([wr_ref[hh].astype(bf16), wi_ref[hh].astype(bf16)], axis=1)
                for hh in range(wr_ref.shape[0])]
        chunk = MIX_ROW_CHUNK
        assert t_rows == 2 * chunk
        n_split = MIX_CH_TILE // MXU_COLS
        n_sub = chunk // MIX_SUB_ROWS
        a_rows = chunk // (len(_RECURRENCE_GROUPS) * n_split)
        never = p < 0
        cv_t, xb_t, h_prev = cv_tail[...], xb_tail[...], h_carry[...]

        def normed(c):
            return _rms_norm(x_ref[pl.ds(c * chunk, chunk), :], g_ref[...]).astype(bf16)

        def half(hn, k, hf):
            return _dot(hn, w_in[k][:, pl.ds(hf * MXU_COLS, MXU_COLS)])

        def tied(carry, part):
            rows = part[chunk - SUBLANES:]
            return jnp.where(never, jnp.concatenate([rows] * n_split, axis=1), carry)

        def gated_conv(c, parts, hf, cv_half, pieces, under):
            lanes = slice(hf * MXU_COLS, (hf + 1) * MXU_COLS)
            for q in pieces:
                sl = slice(q * a_rows, (q + 1) * a_rows)
                part = under()
                v_a, b_a, c_a, g_a = (parts[k][hf][sl] for k in (_V, _B, _C, _GA))
                y_a, cv_half = _group_a_piece(v_a, b_a, c_a, g_a, caw_ref[:, lanes],
                                              cab_ref[:, lanes], cv_half)
                ya_ref[pl.ds(c * chunk + sl.start, a_rows), lanes] = y_a.astype(ya_ref.dtype)
                if part is not None:
                    cv_half = jnp.where(never, part[chunk - SUBLANES:], cv_half)
            return cv_half

        def recurrence(c, hn, xb, g_b, xb_tail_c, h_prev, cv_half, first):
            xc, r_pre, i_pre = _conv_gates(xb, xb_tail_c, cbw_ref[...], cbb_ref[...], w_ri)
            todo = [(k, hf) for hf in range(n_split) for k in _CONV_GROUPS]
            parts = {k: [None] * n_split for k in _CONV_GROUPS}
            n_first = len(_CONV_GROUPS)
            for s in range(n_sub):
                sl = slice(s * MIX_SUB_ROWS, (s + 1) * MIX_SUB_ROWS)
                k, hf = todo.pop(0)
                parts[k][hf] = half(hn, k, hf)
                y_b, h_prev = _group_b_piece(xc[sl], r_pre[sl], i_pre[sl], g_b[sl], br_ref[...],
                                             bi_ref[...], lam_scale, h_prev, first if s == 0 else None)
                yb_ref[pl.ds(c * chunk + sl.start, MIX_SUB_ROWS), :] = y_b.astype(yb_ref.dtype)
                if s >= n_first:
                    share = (chunk // a_rows) // (n_sub - n_first)
                    done = (s - n_first) * share
                    cv_half = gated_conv(c, parts, 0, cv_half, range(done, done + share), lambda: None)
                h_prev = tied(h_prev, parts[k][hf])
            assert not todo
            return parts, h_prev, cv_half

        assert n_split == 2
        cv_halves = [cv_t[:, :MXU_COLS], cv_t[:, MXU_COLS:]]
        head_rows = chunk // 2
        hn0_halves = [_rms_norm(x_ref[pl.ds(r, head_rows), :], g_ref[...]).astype(bf16)
                      for r in (0, head_rows)]
        hn0, hn1 = jnp.concatenate(hn0_halves, axis=0), normed(1)
        xb0, gb0 = (jnp.concatenate([_dot(hn, w_in[k][...]) for hn in hn0_halves], axis=0)
                    for k in _RECURRENCE_GROUPS)
        parts0, h_prev, cv_halves[0] = recurrence(0, hn0, xb0, gb0, xb_t, h_prev, cv_halves[0], i == 0)

        todo1 = [(k, hf) for k in _RECURRENCE_GROUPS for hf in range(n_split)]
        parts1 = {k: [None] * n_split for k in _RECURRENCE_GROUPS}

        def next_recurrence_half():
            k, hf = todo1.pop(0)
            parts1[k][hf] = half(hn1, k, hf)
            return parts1[k][hf]

        all_pieces = range(chunk // a_rows)
        cv_halves[1] = gated_conv(0, parts0, 1, cv_halves[1], all_pieces, next_recurrence_half)
        assert not todo1
        xb1, gb1 = (jnp.concatenate(parts1[k], axis=1) for k in _RECURRENCE_GROUPS)
        parts1a, h_prev, cv_halves[0] = recurrence(1, hn1, xb1, gb1, xb0[chunk - SUBLANES:], h_prev,
                                                   cv_halves[0], None)
        cv_halves[1] = gated_conv(1, parts1a, 1, cv_halves[1], all_pieces, lambda: None)
        cv_tail[...] = jnp.concatenate(cv_halves, axis=1)
        xb_tail[...], h_carry[...] = xb1[chunk - SUBLANES:], h_prev
        cast_w_in_piece()


def _attn_kernel(x_ref, ya_ref, yb_ref, woa_ref, wob_ref, gx_ref, wq_ref, k_ref, v_ref,
                 wo_ref, gf_ref, out_ref):
    h = x_ref[...] + _dot(ya_ref[...], woa_ref[...]) + _dot(yb_ref[...], wob_ref[...])
    hn = _rms_norm(h, gx_ref[...]).astype(jnp.bfloat16)
    q = _dot(hn, wq_ref[...]).astype(jnp.bfloat16)
    scale = XATTN_HEAD_DIM ** -0.5
    heads = [slice(hd * XATTN_HEAD_DIM, (hd + 1) * XATTN_HEAD_DIM) for hd in range(XATTN_HEADS)]
    scores = [lax.dot_general(q[:, sl], k_ref[:, sl], (((1,), (1,)), ((), ())),
                              preferred_element_type=jnp.float32) * scale for sl in heads]
    probs = []
    for s in scores:
        p = jnp.exp(s - jnp.max(s, axis=-1, keepdims=True))
        probs.append((p / jnp.sum(p, axis=-1, keepdims=True)).astype(jnp.bfloat16))
    o_parts = [_dot(p, v_ref[:, sl]) for p, sl in zip(probs, heads)]
    o = jnp.concatenate(o_parts, axis=1).astype(jnp.bfloat16)
    h2 = h + _dot(o, wo_ref[...])
    out_ref[...] = _rms_norm(h2, gf_ref[...]).astype(out_ref.dtype)


def _resident(shape):
    return pl.BlockSpec(shape, lambda *_: (0,) * len(shape), pipeline_mode=pl.Buffered(1))


def _layer(h, mem, norm_mix_g, w_in, conv_a_w, conv_a_b, conv_b_w, conv_b_b, w_rgate, b_rgate,
           w_igate, b_igate, lru_lambda, w_out, norm_x_g, norm_mem_g, w_q, w_kv, w_o, norm_f_g):
    seq, d = h.shape
    n_mem = mem.shape[0]
    bf16 = jnp.bfloat16
    row = lambda v: v.reshape(1, -1)

    tm, cb = MIX_SEQ_TILE, MIX_CH_TILE
    n_cb, n_t = d // cb, seq // tm
    n_steps = n_cb * n_t
    piece = d // n_t
    blk = lambda p: jnp.maximum(p - 1, 0)
    nxt = lambda p: jnp.minimum(p, n_cb - 1)
    step = lambda p, i: jnp.minimum(p * n_t + i, n_steps - 1)
    tile = lambda p, i: jnp.where(p > 0, i, 0)
    col = lambda g: pl.BlockSpec((piece, cb), lambda p, i, g=g: (i, g * n_cb + nxt(p)))
    chan = lambda r: pl.BlockSpec((r, cb), lambda p, i: (0, blk(p)))
    gate = pl.BlockSpec((cb // HEAD_DIM, HEAD_DIM, HEAD_DIM), lambda p, i: (blk(p), 0, 0))
    y_spec = pl.BlockSpec((tm, cb), lambda p, i: (tile(p, i), blk(p)))
    rows_of = lambda w: pl.BlockSpec((w.shape[0] // n_steps, d), lambda p, i: (step(p, i), 0))
    kv_cols = 2 * d // n_t
    kv_blk = lambda p, i: (0, jnp.where(p > 0, n_t - 1, i))
    y_a, y_b, w_out_bf, w_q_bf, w_o_bf, kv = pl.pallas_call(
        _mixer_kernel,
        grid=(n_cb + 1, n_t),
        in_specs=[pl.BlockSpec((tm, d), lambda p, i: (tile(p, i), 0)),
                  pl.BlockSpec((1, d), lambda p, i: (0, 0)),
                  col(0), col(1), col(2), col(3), col(4), col(5),
                  chan(conv_a_w.shape[0]), chan(1), chan(conv_b_w.shape[0]), chan(1),
                  gate, gate, chan(1), chan(1), chan(1),
                  rows_of(w_out), rows_of(w_q), rows_of(w_o),
                  _resident((n_mem, d)), _resident((1, d)), pl.BlockSpec((d, kv_cols), kv_blk)],
        out_specs=[y_spec, y_spec, rows_of(w_out), rows_of(w_q), rows_of(w_o),
                   pl.BlockSpec((n_mem, kv_cols), kv_blk)],
        out_shape=[jax.ShapeDtypeStruct((seq, d), bf16), jax.ShapeDtypeStruct((seq, d), bf16),
                   jax.ShapeDtypeStruct(w_out.shape, bf16), jax.ShapeDtypeStruct(w_q.shape, bf16),
                   jax.ShapeDtypeStruct(w_o.shape, bf16), jax.ShapeDtypeStruct((n_mem, 2 * d), bf16)],
        scratch_shapes=[pltpu.VMEM((2, 6, d, cb), bf16),
                        pltpu.VMEM((SUBLANES, cb), jnp.float32),
                        pltpu.VMEM((SUBLANES, cb), jnp.float32),
                        pltpu.VMEM((SUBLANES, cb), jnp.float32)],
        compiler_params=pltpu.CompilerParams(dimension_semantics=("arbitrary", "arbitrary"),
                                             vmem_limit_bytes=VMEM_LIMIT_BYTES),
        name="mixer",
    )(h, row(norm_mix_g), w_in, w_in, w_in, w_in, w_in, w_in,
      conv_a_w, row(conv_a_b), conv_b_w, row(conv_b_b), w_rgate, w_igate,
      row(b_rgate), row(b_igate), row(lru_lambda), w_out, w_q, w_o, mem, row(norm_mem_g), w_kv)

    ta = ATTN_SEQ_TILE
    out = pl.pallas_call(
        _attn_kernel,
        grid=(seq // ta,),
        in_specs=[pl.BlockSpec((ta, d), lambda i: (i, 0)),
                  pl.BlockSpec((ta, d), lambda i: (i, 0)),
                  pl.BlockSpec((ta, d), lambda i: (i, 0)),
                  pl.BlockSpec((d, d), lambda i: (0, 0), pipeline_mode=pl.Buffered(1)),
                  pl.BlockSpec((d, d), lambda i: (1, 0), pipeline_mode=pl.Buffered(1)),
                  _resident((1, d)),
                  _resident((d, d)),
                  pl.BlockSpec((n_mem, d), lambda i: (0, 0), pipeline_mode=pl.Buffered(1)),
                  pl.BlockSpec((n_mem, d), lambda i: (0, 1), pipeline_mode=pl.Buffered(1)),
                  _resident((d, d)),
                  _resident((1, d))],
        out_specs=pl.BlockSpec((ta, d), lambda i: (i, 0)),
        out_shape=jax.ShapeDtypeStruct((seq, d), h.dtype),
        compiler_params=pltpu.CompilerParams(dimension_semantics=("arbitrary",),
                                             vmem_limit_bytes=VMEM_LIMIT_BYTES),
        name="outproj_xattn",
    )(h, y_a, y_b, w_out_bf, w_out_bf, row(norm_x_g), w_q_bf, kv, kv, w_o_bf, row(norm_f_g))
    return out


def kernel(x, mem, norm_mix_g, w_in, conv_a_w, conv_a_b, conv_b_w, conv_b_b, w_rgate, b_rgate,
           w_igate, b_igate, lru_lambda, w_out, norm_x_g, norm_mem_g, w_q, w_kv, w_o, norm_f_g):
    bsz, seq, d = x.shape
    depth = w_in.shape[0]
    assert bsz == 1 and depth == 1 and d == D_MODEL
    out = _layer(x[0], mem[0], norm_mix_g[0], w_in[0], conv_a_w[0], conv_a_b[0], conv_b_w[0],
                 conv_b_b[0], w_rgate[0], b_rgate[0], w_igate[0], b_igate[0], lru_lambda[0],
                 w_out[0], norm_x_g[0], norm_mem_g[0], w_q[0], w_kv[0], w_o[0], norm_f_g)
    return out[None]
```

```python
import jax
import jax.numpy as jnp
from jax import lax
from jax.experimental import pallas as pl
from jax.experimental.pallas import tpu as pltpu

D_MODEL = 2048
HEAD_DIM = 128
LRU_C = 8.0
XATTN_HEADS = 4
XATTN_HEAD_DIM = D_MODEL // XATTN_HEADS
RMS_EPS = 1e-6

SUBLANES = 8
MXU_COLS = 256
VMEM_LIMIT_BYTES = 56 * 1024 * 1024

MIX_SEQ_TILE = 512
MIX_ROW_CHUNK = 256
MIX_SUB_ROWS = 32
_V, _B, _C, _GA, _XB, _GB = range(6)
_RECURRENCE_GROUPS = (_XB, _GB)
_CONV_GROUPS = (_C, _V, _B, _GA)
MIX_CH_TILE = 512
ATTN_SEQ_TILE = 256


def _rms_norm(xf, g):
    ms = jnp.mean(xf * xf, axis=-1, keepdims=True)
    return xf * lax.rsqrt(ms + RMS_EPS) * g


def _dot(a, b):
    return jnp.dot(a, b, preferred_element_type=jnp.float32)


def _group_rows(u):
    return u.reshape(u.shape[0] // SUBLANES, SUBLANES, u.shape[1])


def _shift_rows(u, tail, s):
    rows, ch = u.shape
    rolled = pltpu.roll(_group_rows(u), s, axis=1)
    before = jnp.concatenate([pltpu.roll(tail, s, axis=0)[None], rolled[:-1]], axis=0)
    row = lax.broadcasted_iota(jnp.int32, rolled.shape, 1)
    return jnp.where(row < s, before, rolled).reshape(rows, ch)


def _linear_scan(a, b, h_prev):
    t_rows, ch = a.shape
    a, b = _group_rows(a), _group_rows(b)
    row = lax.broadcasted_iota(jnp.int32, a.shape, 1)
    for s in (1, 2, 4):
        keep = row >= s
        b = a * jnp.where(keep, pltpu.roll(b, s, axis=1), 0.0) + b
        a = a * jnp.where(keep, pltpu.roll(a, s, axis=1), 1.0)
    outs = []
    for g in range(t_rows // SUBLANES):
        h_g = a[g] * h_prev + b[g]
        outs.append(h_g)
        h_prev = jnp.broadcast_to(h_g[SUBLANES - 1:SUBLANES, :], (SUBLANES, ch))
    return jnp.concatenate(outs, axis=0), h_prev


def _silu(g):
    return g * jax.nn.sigmoid(g)


def _conv_gates(xb, xb_tail, cbw, cbb, w_ri):
    xc = (cbb + cbw[3:4] * xb
          + cbw[2:3] * _shift_rows(xb, xb_tail, 1)
          + cbw[1:2] * _shift_rows(xb, xb_tail, 2)
          + cbw[0:1] * _shift_rows(xb, xb_tail, 3))
    xc_bf = xc.astype(jnp.bfloat16)
    r_parts, i_parts = [], []
    for hh in range(xb.shape[1] // HEAD_DIM):
        ri = _dot(xc_bf[:, hh * HEAD_DIM:(hh + 1) * HEAD_DIM], w_ri[hh])
        r_parts.append(ri[:, :HEAD_DIM])
        i_parts.append(ri[:, HEAD_DIM:])
    return xc, jnp.concatenate(r_parts, axis=1), jnp.concatenate(i_parts, axis=1)


def _group_a_piece(v_a, b_a, c_a, g_a, caw, cab, cv_tail):
    cv = c_a * v_a
    conv = (cab + caw[2:3] * cv
            + caw[1:2] * _shift_rows(cv, cv_tail, 1)
            + caw[0:1] * _shift_rows(cv, cv_tail, 2))
    return b_a * conv * _silu(g_a), cv[cv.shape[0] - SUBLANES:]


def _group_b_piece(xc, r_pre, i_pre, g_b, b_r, b_i, lam_scale, h_prev, is_sequence_start):
    r = jax.nn.sigmoid(r_pre + b_r)
    i_gate = jax.nn.sigmoid(i_pre + b_i)
    log_a = lam_scale * r
    a = jnp.exp(log_a)
    th = jnp.tanh(log_a)
    mult = jnp.sqrt(-2.0 * th / (1.0 - th))
    if is_sequence_start is not None:
        first_row = (lax.broadcasted_iota(jnp.int32, mult.shape, 0) == 0) & is_sequence_start
        mult = jnp.where(first_row, 1.0, mult)
    h, h_prev = _linear_scan(a, mult * i_gate * xc, h_prev)
    return h * _silu(g_b), h_prev


def _mixer_kernel(x_ref, g_ref, wv_ref, wb_ref, wc_ref, wga_ref, wxb_ref, wgb_ref,
                  caw_ref, cab_ref, cbw_ref, cbb_ref, wr_ref, wi_ref, br_ref, bi_ref, lam_ref,
                  wout_ref, wq_ref, wo_ref, mem_ref, gm_ref, wkv_ref,
                  ya_ref, yb_ref, wout_bf_ref, wq_bf_ref, wo_bf_ref, kv_ref,
                  win_bf, cv_tail, xb_tail, h_carry):
    p, i = pl.program_id(0), pl.program_id(1)
    t_rows = ya_ref.shape[0]
    bf16 = jnp.bfloat16

    def cast_riders():
        for src, dst in ((wout_ref, wout_bf_ref), (wq_ref, wq_bf_ref), (wo_ref, wo_bf_ref)):
            dst[...] = src[...].astype(bf16)

    def cast_w_in_piece():
        piece = wv_ref.shape[0]
        w_next = win_bf.at[p % 2]
        for g, w_ref in enumerate((wv_ref, wb_ref, wc_ref, wga_ref, wxb_ref, wgb_ref)):
            w_next[g, pl.ds(i * piece, piece), :] = w_ref[...].astype(bf16)

    @pl.when(p == 0)
    def _():
        cast_w_in_piece()
        mn = _rms_norm(mem_ref[...], gm_ref[...]).astype(bf16)
        kv_ref[...] = _dot(mn, wkv_ref[...].astype(bf16)).astype(kv_ref.dtype)

    @pl.when(p > 0)
    def _():
        @pl.when(i == 0)
        def _():
            cv_tail[...] = jnp.zeros_like(cv_tail)
            xb_tail[...] = jnp.zeros_like(xb_tail)
            h_carry[...] = jnp.zeros_like(h_carry)

        cast_riders()
        neg_lam = -lam_ref[...]
        softplus_neg_lam = jnp.maximum(neg_lam, 0.0) + jnp.log1p(jnp.exp(-jnp.abs(neg_lam)))
        lam_scale = -LRU_C * softplus_neg_lam
        w_cur = win_bf.at[(p - 1) % 2]
        w_in = [w_cur.at[g] for g in range(6)]
        w_ri = [jnp.concatenate([wr_ref[hh].astype(bf16), wi_ref[hh].astype(bf16)], axis=1)
                for hh in range(wr_ref.shape[0])]
        chunk = MIX_ROW_CHUNK
        assert t_rows == 2 * chunk
        n_split = MIX_CH_TILE // MXU_COLS
        n_sub = chunk // MIX_SUB_ROWS
        a_rows = chunk // (len(_RECURRENCE_GROUPS) * n_split)
        never = p < 0
        cv_t, xb_t, h_prev = cv_tail[...], xb_tail[...], h_carry[...]

        def normed(c):
            return _rms_norm(x_ref[pl.ds(c * chunk, chunk), :], g_ref[...]).astype(bf16)

        def half(hn, k, hf):
            return _dot(hn, w_in[k][:, pl.ds(hf * MXU_COLS, MXU_COLS)])

        def tied(carry, part):
            rows = part[chunk - SUBLANES:]
            return jnp.where(never, jnp.concatenate([rows] * n_split, axis=1), carry)

        def gated_conv(c, parts, hf, cv_half, pieces, under):
            lanes = slice(hf * MXU_COLS, (hf + 1) * MXU_COLS)
            for q in pieces:
                sl = slice(q * a_rows, (q + 1) * a_rows)
                part = under()
                v_a, b_a, c_a, g_a = (parts[k][hf][sl] for k in (_V, _B, _C, _GA))
                y_a, cv_half = _group_a_piece(v_a, b_a, c_a, g_a, caw_ref[:, lanes],
                                              cab_ref[:, lanes], cv_half)
                ya_ref[pl.ds(c * chunk + sl.start, a_rows), lanes] = y_a.astype(ya_ref.dtype)
                if part is not None:
                    cv_half = jnp.where(never, part[chunk - SUBLANES:], cv_half)
            return cv_half

        def recurrence(c, hn, xb, g_b, xb_tail_c, h_prev, cv_half, first):
            xc, r_pre, i_pre = _conv_gates(xb, xb_tail_c, cbw_ref[...], cbb_ref[...], w_ri)
            todo = [(k, hf) for hf in range(n_split) for k in _CONV_GROUPS]
            parts = {k: [None] * n_split for k in _CONV_GROUPS}
            n_first = len(_CONV_GROUPS)
            for s in range(n_sub):
                sl = slice(s * MIX_SUB_ROWS, (s + 1) * MIX_SUB_ROWS)
                k, hf = todo.pop(0)
                parts[k][hf] = half(hn, k, hf)
                y_b, h_prev = _group_b_piece(xc[sl], r_pre[sl], i_pre[sl], g_b[sl], br_ref[...],
                                             bi_ref[...], lam_scale, h_prev, first if s == 0 else None)
                yb_ref[pl.ds(c * chunk + sl.start, MIX_SUB_ROWS), :] = y_b.astype(yb_ref.dtype)
                if s >= n_first:
                    share = (chunk // a_rows) // (n_sub - n_first)
                    done = (s - n_first) * share
                    cv_half = gated_conv(c, parts, 0, cv_half, range(done, done + share), lambda: None)
                h_prev = tied(h_prev, parts[k][hf])
            assert not todo
            return parts, h_prev, cv_half

        assert n_split == 2
        cv_halves = [cv_t[:, :MXU_COLS], cv_t[:, MXU_COLS:]]
        head_rows = chunk // 2
        hn0_halves = [_rms_norm(x_ref[pl.ds(r, head_rows), :], g_ref[...]).astype(bf16)
                      for r in (0, head_rows)]
        hn0, hn1 = jnp.concatenate(hn0_halves, axis=0), normed(1)
        xb0, gb0 = (jnp.concatenate([_dot(hn, w_in[k][...]) for hn in hn0_halves], axis=0)
                    for k in _RECURRENCE_GROUPS)
        parts0, h_prev, cv_halves[0] = recurrence(0, hn0, xb0, gb0, xb_t, h_prev, cv_halves[0], i == 0)

        todo1 = [(k, hf) for k in _RECURRENCE_GROUPS for hf in range(n_split)]
        parts1 = {k: [None] * n_split for k in _RECURRENCE_GROUPS}

        def next_recurrence_half():
            k, hf = todo1.pop(0)
            parts1[k][hf] = half(hn1, k, hf)
            return parts1[k][hf]

        all_pieces = range(chunk // a_rows)
        cv_halves[1] = gated_conv(0, parts0, 1, cv_halves[1], all_pieces, next_recurrence_half)
        assert not todo1
        xb1, gb1 = (jnp.concatenate(parts1[k], axis=1) for k in _RECURRENCE_GROUPS)
        parts1a, h_prev, cv_halves[0] = recurrence(1, hn1, xb1, gb1, xb0[chunk - SUBLANES:], h_prev,
                                                   cv_halves[0], None)
        cv_halves[1] = gated_conv(1, parts1a, 1, cv_halves[1], all_pieces, lambda: None)
        cv_tail[...] = jnp.concatenate(cv_halves, axis=1)
        xb_tail[...], h_carry[...] = xb1[chunk - SUBLANES:], h_prev
        cast_w_in_piece()


def _attn_kernel(x_ref, ya_ref, yb_ref, woa_ref, wob_ref, gx_ref, wq_ref, k_ref, v_ref,
                 wo_ref, gf_ref, out_ref):
    h = x_ref[...] + _dot(ya_ref[...], woa_ref[...]) + _dot(yb_ref[...], wob_ref[...])
    hn = _rms_norm(h, gx_ref[...]).astype(jnp.bfloat16)
    q = _dot(hn, wq_ref[...]).astype(jnp.bfloat16)
    scale = XATTN_HEAD_DIM ** -0.5
    heads = [slice(hd * XATTN_HEAD_DIM, (hd + 1) * XATTN_HEAD_DIM) for hd in range(XATTN_HEADS)]
    scores = [lax.dot_general(q[:, sl], k_ref[:, sl], (((1,), (1,)), ((), ())),
                              preferred_element_type=jnp.float32) * scale for sl in heads]
    probs = []
    for s in scores:
        p = jnp.exp(s - jnp.max(s, axis=-1, keepdims=True))
        probs.append((p / jnp.sum(p, axis=-1, keepdims=True)).astype(jnp.bfloat16))
    o_parts = [_dot(p, v_ref[:, sl]) for p, sl in zip(probs, heads)]
    o = jnp.concatenate(o_parts, axis=1).astype(jnp.bfloat16)
    h2 = h + _dot(o, wo_ref[...])
    out_ref[...] = _rms_norm(h2, gf_ref[...]).astype(out_ref.dtype)


def _resident(shape):
    return pl.BlockSpec(shape, lambda *_: (0,) * len(shape), pipeline_mode=pl.Buffered(1))


def _layer(h, mem, norm_mix_g, w_in, conv_a_w, conv_a_b, conv_b_w, conv_b_b, w_rgate, b_rgate,
           w_igate, b_igate, lru_lambda, w_out, norm_x_g, norm_mem_g, w_q, w_kv, w_o, norm_f_g):
    seq, d = h.shape
    n_mem = mem.shape[0]
    bf16 = jnp.bfloat16
    row = lambda v: v.reshape(1, -1)

    tm, cb = MIX_SEQ_TILE, MIX_CH_TILE
    n_cb, n_t = d // cb, seq // tm
    n_steps = n_cb * n_t
    piece = d // n_t
    blk = lambda p: jnp.maximum(p - 1, 0)
    nxt = lambda p: jnp.minimum(p, n_cb - 1)
    step = lambda p, i: jnp.where(p > 0, (p - 1) * n_t + i, 0)
    tile = lambda p, i: jnp.where(p > 0, i, 0)
    col = lambda g: pl.BlockSpec((piece, cb), lambda p, i, g=g: (i, g * n_cb + nxt(p)))
    chan = lambda r: pl.BlockSpec((r, cb), lambda p, i: (0, blk(p)))
    gate = pl.BlockSpec((cb // HEAD_DIM, HEAD_DIM, HEAD_DIM), lambda p, i: (blk(p), 0, 0))
    y_spec = pl.BlockSpec((tm, cb), lambda p, i: (tile(p, i), blk(p)))
    rows_of = lambda w: pl.BlockSpec((w.shape[0] // n_steps, d), lambda p, i: (step(p, i), 0))
    kv_cols = 2 * d // n_t
    kv_blk = lambda p, i: (0, jnp.where(p > 0, n_t - 1, i))
    y_a, y_b, w_out_bf, w_q_bf, w_o_bf, kv = pl.pallas_call(
        _mixer_kernel,
        grid=(n_cb + 1, n_t),
        in_specs=[pl.BlockSpec((tm, d), lambda p, i: (tile(p, i), 0)),
                  pl.BlockSpec((1, d), lambda p, i: (0, 0)),
                  col(0), col(1), col(2), col(3), col(4), col(5),
                  chan(conv_a_w.shape[0]), chan(1), chan(conv_b_w.shape[0]), chan(1),
                  gate, gate, chan(1), chan(1), chan(1),
                  rows_of(w_out), rows_of(w_q), rows_of(w_o),
                  _resident((n_mem, d)), _resident((1, d)), pl.BlockSpec((d, kv_cols), kv_blk)],
        out_specs=[y_spec, y_spec, rows_of(w_out), rows_of(w_q), rows_of(w_o),
                   pl.BlockSpec((n_mem, kv_cols), kv_blk)],
        out_shape=[jax.ShapeDtypeStruct((seq, d), bf16), jax.ShapeDtypeStruct((seq, d), bf16),
                   jax.ShapeDtypeStruct(w_out.shape, bf16), jax.ShapeDtypeStruct(w_q.shape, bf16),
                   jax.ShapeDtypeStruct(w_o.shape, bf16), jax.ShapeDtypeStruct((n_mem, 2 * d), bf16)],
        scratch_shapes=[pltpu.VMEM((2, 6, d, cb), bf16),
                        pltpu.VMEM((SUBLANES, cb), jnp.float32),
                        pltpu.VMEM((SUBLANES, cb), jnp.float32),
                        pltpu.VMEM((SUBLANES, cb), jnp.float32)],
        compiler_params=pltpu.CompilerParams(dimension_semantics=("arbitrary", "arbitrary"),
                                             vmem_limit_bytes=VMEM_LIMIT_BYTES),
        name="mixer",
    )(h, row(norm_mix_g), w_in, w_in, w_in, w_in, w_in, w_in,
      conv_a_w, row(conv_a_b), conv_b_w, row(conv_b_b), w_rgate, w_igate,
      row(b_rgate), row(b_igate), row(lru_lambda), w_out, w_q, w_o, mem, row(norm_mem_g), w_kv)

    ta = ATTN_SEQ_TILE
    out = pl.pallas_call(
        _attn_kernel,
        grid=(seq // ta,),
        in_specs=[pl.BlockSpec((ta, d), lambda i: (i, 0)),
                  pl.BlockSpec((ta, d), lambda i: (i, 0)),
                  pl.BlockSpec((ta, d), lambda i: (i, 0)),
                  pl.BlockSpec((d, d), lambda i: (0, 0), pipeline_mode=pl.Buffered(1)),
                  pl.BlockSpec((d, d), lambda i: (1, 0), pipeline_mode=pl.Buffered(1)),
                  _resident((1, d)),
                  _resident((d, d)),
                  pl.BlockSpec((n_mem, d), lambda i: (0, 0), pipeline_mode=pl.Buffered(1)),
                  pl.BlockSpec((n_mem, d), lambda i: (0, 1), pipeline_mode=pl.Buffered(1)),
                  _resident((d, d)),
                  _resident((1, d))],
        out_specs=pl.BlockSpec((ta, d), lambda i: (i, 0)),
        out_shape=jax.ShapeDtypeStruct((seq, d), h.dtype),
        compiler_params=pltpu.CompilerParams(dimension_semantics=("arbitrary",),
                                             vmem_limit_bytes=VMEM_LIMIT_BYTES),
        name="outproj_xattn",
    )(h, y_a, y_b, w_out_bf, w_out_bf, row(norm_x_g), w_q_bf, kv, kv, w_o_bf, row(norm_f_g))
    return out


def kernel(x, mem, norm_mix_g, w_in, conv_a_w, conv_a_b, conv_b_w, conv_b_b, w_rgate, b_rgate,
           w_igate, b_igate, lru_lambda, w_out, norm_x_g, norm_mem_g, w_q, w_kv, w_o, norm_f_g):
    bsz, seq, d = x.shape
    depth = w_in.shape[0]
    assert bsz == 1 and depth == 1 and d == D_MODEL
    out = _layer(x[0], mem[0], norm_mix_g[0], w_in[0], conv_a_w[0], conv_a_b[0], conv_b_w[0],
                 conv_b_b[0], w_rgate[0], b_rgate[0], w_igate[0], b_igate[0], lru_lambda[0],
                 w_out[0], norm_x_g[0], norm_mem_g[0], w_q[0], w_kv[0], w_o[0], norm_f_g)
    return out[None]
```
